```python
import math
import jax
import jax.numpy as jnp
from jax import lax
import numpy as np

D_MODEL = 1024
BATCH = 16
SEQ = 256
DEPTH = 4
DEC_BATCH = 8
DEC_SEQ = 2048
PAST_LEN = 256

GRID_W = 64
N_MIXERS = 4
N_LAYERS_MLA = len(range(0, DEPTH, N_MIXERS))
N_LAYERS_DIFF = len(range(1, DEPTH, N_MIXERS))
N_LAYERS_SCONV = len(range(2, DEPTH, N_MIXERS))
N_LAYERS_GMLP = len(range(3, DEPTH, N_MIXERS))
EPS = 1e-6
ROPE_THETA = 10000.0
Q_BLOCK = 128
MLA_HEADS = 8
MLA_NOPE = 128
MLA_ROPE = 64
MLA_V = 128
MLA_Q_LORA = 768
MLA_KV_LORA = 256
DIFF_HEADS = 8
DIFF_HD = D_MODEL // (2 * DIFF_HEADS)
GMLP_WIDTH = D_MODEL
GMLP_CHUNK = 128
GMLP_GROUPS = 8
FFN_HIDDEN = 2816
N_MOD = 6

kernel_name = "hybrid_diffusion_prefix_step"


def _rms(x, g):
    xf = x.astype(jnp.float32)
    y = xf * lax.rsqrt(jnp.mean(xf * xf, axis=-1, keepdims=True) + EPS)
    return (y * g.astype(jnp.float32)).astype(x.dtype)


def _modulate(x, g, shift, scale):
    return _rms(x, g) * (1 + scale) + shift


def _axial_rope_tables(s, rope_dim):
    rows = s // GRID_W
    row = jnp.repeat(jnp.arange(rows, dtype=jnp.float32), GRID_W)
    col = jnp.tile(jnp.arange(GRID_W, dtype=jnp.float32), rows)
    n_freq = rope_dim // 4
    inv_freq = ROPE_THETA ** (-jnp.arange(n_freq, dtype=jnp.float32) / n_freq)
    ang = jnp.concatenate([row[:, None] * inv_freq, col[:, None] * inv_freq], axis=-1)
    return jnp.cos(ang), jnp.sin(ang)


def _rope(x, cos, sin):
    s, half = cos.shape
    bshape = (1, s) + (1,) * (x.ndim - 3) + (half,)
    cos = cos.reshape(bshape)
    sin = sin.reshape(bshape)
    xf = x.astype(jnp.float32)
    x1, x2 = xf[..., :half], xf[..., half:]
    return jnp.concatenate([x1 * cos - x2 * sin, x2 * cos + x1 * sin], axis=-1).astype(x.dtype)


def _dwconv3(x, w):
    xp = jnp.pad(x, ((0, 0), (1, 1), (0, 0)))
    return xp[:, :-2] * w[0] + xp[:, 1:-1] * w[1] + xp[:, 2:] * w[2]


def _attend(q, k, v):
    b, sq, g, h, d = q.shape
    nb = sq // Q_BLOCK
    scale = d ** -0.5
    qb = jnp.moveaxis(q.reshape(b, nb, Q_BLOCK, g, h, d), 1, 0)

    def block(qi):
        s = jnp.einsum('bqghd,bkghd->bghqk', qi, k).astype(jnp.float32) * scale
        p = jax.nn.softmax(s, axis=-1).astype(v.dtype)
        return jnp.einsum('bghqk,bkhe->bqghe', p, v)

    o = lax.map(block, qb)
    return jnp.moveaxis(o, 0, 1).reshape(b, sq, g, h, v.shape[-1])


def _mla(h, rope, ctx, w_down, q_norm, kv_norm, w_uq, w_uk, w_uv, qn_nope, qn_rope, kn_nope, kn_rope, w_o):
    b, s, _ = h.shape
    c_q, c_kv, k_rope = jnp.split(h @ w_down, [MLA_Q_LORA, MLA_Q_LORA + MLA_KV_LORA], axis=-1)
    q = (_rms(c_q, q_norm) @ w_uq).reshape(b, s, MLA_HEADS, MLA_NOPE + MLA_ROPE)
    q_nope = _rms(q[..., :MLA_NOPE], qn_nope)
    q_rope = _rms(q[..., MLA_NOPE:], qn_rope)
    c_kv = _rms(c_kv, kv_norm)
    k_rope = _rms(k_rope, kn_rope)
    if rope is None:
        ckv_all, kr_all = c_kv, k_rope
    else:
        q_rope = _rope(q_rope, *rope)
        ckv_all = jnp.concatenate([ctx[0], c_kv], axis=1)
        kr_all = jnp.concatenate([ctx[1], _rope(k_rope, *rope)], axis=1)
    l = ckv_all.shape[1]
    k_nope = _rms((ckv_all @ w_uk).reshape(b, l, MLA_HEADS, MLA_NOPE), kn_nope)
    k_pe = jnp.broadcast_to(kr_all[:, :, None, :], (b, l, MLA_HEADS, MLA_ROPE))
    k = jnp.concatenate([k_nope, k_pe], axis=-1)[:, :, None]
    v = (ckv_all @ w_uv).reshape(b, l, MLA_HEADS, MLA_V)
    qf = jnp.concatenate([q_nope, q_rope], axis=-1)[:, :, None]
    o = _attend(qf, k, v).reshape(b, s, MLA_HEADS * MLA_V)
    return o @ w_o, c_kv, k_rope


def _diff(h, rope, ctx, lam_init, w_qkv, qn, kn, lq1, lk1, lq2, lk2, head_norm, w_o):
    b, s, _ = h.shape
    q, k, v = jnp.split(h @ w_qkv, 3, axis=-1)
    q = _rms(q.reshape(b, s, 2, DIFF_HEADS, DIFF_HD), qn)
    k = _rms(k.reshape(b, s, 2, DIFF_HEADS, DIFF_HD), kn)
    v = v.reshape(b, s, DIFF_HEADS, 2 * DIFF_HD)
    if rope is None:
        k_all, v_all = k, v
    else:
        q = _rope(q, *rope)
        k_all = jnp.concatenate([ctx[0], _rope(k, *rope)], axis=1)
        v_all = jnp.concatenate([ctx[1], v], axis=1)
    o = _attend(q, k_all, v_all)
    lam = (jnp.exp(jnp.sum(lq1.astype(jnp.float32) * lk1.astype(jnp.float32)))
           - jnp.exp(jnp.sum(lq2.astype(jnp.float32) * lk2.astype(jnp.float32))) + lam_init).astype(o.dtype)
    o = o[:, :, 0] - lam * o[:, :, 1]
    o = _rms(o, head_norm) * (1.0 - lam_init)
    return o.reshape(b, s, D_MODEL) @ w_o, k, v


def _short_conv(h, w_in, conv_w, w_out):
    gb, gc, u = jnp.split(h @ w_in, 3, axis=-1)
    return (gb * _dwconv3(gc * u, conv_w)) @ w_out


def _chunk_gmlp(h, w_in, v_norm, w_s, b_s, w_out):
    b, s, _ = h.shape
    u, v = jnp.split(jax.nn.gelu(h @ w_in), 2, axis=-1)
    v = _rms(v, v_norm).reshape(b, s // GMLP_CHUNK, GMLP_CHUNK, GMLP_GROUPS, GMLP_WIDTH // GMLP_GROUPS)
    mixed = jnp.einsum('gpq,bcqge->bcpge', w_s, v) + b_s.T[None, None, :, :, None]
    return (u * mixed.reshape(b, s, GMLP_WIDTH)) @ w_out


def _conv_ffn(h, w_in, conv_w, conv_b, w_out):
    a = _dwconv3(h @ w_in, conv_w) + conv_b
    g, up = jnp.split(a, 2, axis=-1)
    return (jax.nn.silu(g) * up) @ w_out


def setup_inputs(seed: int = 0) -> dict:
    key = jax.random.key(seed)
    ks = iter(jax.random.split(key, 64))

    def nrm(shape, scale=1.0):
        return jax.random.normal(next(ks), shape, jnp.float32) * scale

    def gain(shape):
        return 1.0 + nrm(shape, 0.02)

    D = D_MODEL
    nA, nB, nC, nD = N_LAYERS_MLA, N_LAYERS_DIFF, N_LAYERS_SCONV, N_LAYERS_GMLP
    return {
        "x_prompt": nrm((BATCH, SEQ, D)),
        "x_sample": nrm((DEC_BATCH, DEC_SEQ, D)),
        "cache_mla_ckv": nrm((DEC_BATCH, nA, PAST_LEN, MLA_KV_LORA)),
        "cache_mla_krope": nrm((DEC_BATCH, nA, PAST_LEN, MLA_ROPE)),
        "cache_diff_k": nrm((DEC_BATCH, nB, PAST_LEN, 2, DIFF_HEADS, DIFF_HD)),
        "cache_diff_v": nrm((DEC_BATCH, nB, PAST_LEN, DIFF_HEADS, 2 * DIFF_HD)),
        "c": nrm((DEC_BATCH, D)),
        "c_ctx": nrm((D,)),
        "ada_w": nrm((DEPTH, D, N_MOD * D), 0.5 * D ** -0.5),
        "ada_b": nrm((DEPTH, N_MOD * D), 0.02),
        "norm1_g": gain((DEPTH, D)),
        "norm2_g": gain((DEPTH, D)),
        "mla_w_down": nrm((nA, D, MLA_Q_LORA + MLA_KV_LORA + MLA_ROPE), D ** -0.5),
        "mla_q_norm": gain((nA, MLA_Q_LORA)),
        "mla_kv_norm": gain((nA, MLA_KV_LORA)),
        "mla_w_uq": nrm((nA, MLA_Q_LORA, MLA_HEADS * (MLA_NOPE + MLA_ROPE)), MLA_Q_LORA ** -0.5),
        "mla_w_uk": nrm((nA, MLA_KV_LORA, MLA_HEADS * MLA_NOPE), MLA_KV_LORA ** -0.5),
        "mla_w_uv": nrm((nA, MLA_KV_LORA, MLA_HEADS * MLA_V), MLA_KV_LORA ** -0.5),
        "mla_qn_nope": gain((nA, MLA_NOPE)),
        "mla_qn_rope": gain((nA, MLA_ROPE)),
        "mla_kn_nope": gain((nA, MLA_NOPE)),
        "mla_kn_rope": gain((nA, MLA_ROPE)),
        "mla_w_o": nrm((nA, MLA_HEADS * MLA_V, D), (MLA_HEADS * MLA_V) ** -0.5),
        "diff_w_qkv": nrm((nB, D, 3 * D), D ** -0.5),
        "diff_qn": gain((nB, DIFF_HD)),
        "diff_kn": gain((nB, DIFF_HD)),
        "diff_lq1": nrm((nB, DIFF_HD), 0.1),
        "diff_lk1": nrm((nB, DIFF_HD), 0.1),
        "diff_lq2": nrm((nB, DIFF_HD), 0.1),
        "diff_lk2": nrm((nB, DIFF_HD), 0.1),
        "diff_head_norm": gain((nB, 2 * DIFF_HD)),
        "diff_w_o": nrm((nB, D, D), D ** -0.5),
        "sconv_w_in": nrm((nC, D, 3 * D), D ** -0.5),
        "sconv_w": nrm((nC, 3, D), 3 ** -0.5),
        "sconv_w_out": nrm((nC, D, D), D ** -0.5),
        "gmlp_w_in": nrm((nD, D, 2 * GMLP_WIDTH), D ** -0.5),
        "gmlp_v_norm": gain((nD, GMLP_WIDTH)),
        "gmlp_w_s": nrm((nD, GMLP_GROUPS, GMLP_CHUNK, GMLP_CHUNK), GMLP_CHUNK ** -0.5),
        "gmlp_b_s": 1.0 + nrm((nD, GMLP_GROUPS, GMLP_CHUNK), 0.01),
        "gmlp_w_out": nrm((nD, GMLP_WIDTH, D), GMLP_WIDTH ** -0.5),
        "ffn_w_in": nrm((DEPTH, D, 2 * FFN_HIDDEN), D ** -0.5),
        "ffn_conv_w": nrm((DEPTH, 3, 2 * FFN_HIDDEN), 3 ** -0.5),
        "ffn_conv_b": nrm((DEPTH, 2 * FFN_HIDDEN), 0.02),
        "ffn_w_out": nrm((DEPTH, FFN_HIDDEN, D), FFN_HIDDEN ** -0.5),
    }


def reference(x_prompt, x_sample, cache_mla_ckv, cache_mla_krope, cache_diff_k, cache_diff_v, c,
              c_ctx, ada_w, ada_b, norm1_g, norm2_g,
              mla_w_down, mla_q_norm, mla_kv_norm, mla_w_uq, mla_w_uk, mla_w_uv,
              mla_qn_nope, mla_qn_rope, mla_kn_nope, mla_kn_rope, mla_w_o,
              diff_w_qkv, diff_qn, diff_kn, diff_lq1, diff_lk1, diff_lq2, diff_lk2, diff_head_norm, diff_w_o,
              sconv_w_in, sconv_w, sconv_w_out,
              gmlp_w_in, gmlp_v_norm, gmlp_w_s, gmlp_b_s, gmlp_w_out,
              ffn_w_in, ffn_conv_w, ffn_conv_b, ffn_w_out):
    s_lat = x_sample.shape[1]
    rope_mla = _axial_rope_tables(s_lat, MLA_ROPE)
    rope_diff = _axial_rope_tables(s_lat, DIFF_HD)
    cond_ctx = jax.nn.silu(c_ctx)[None, :]
    cond_lat = jax.nn.silu(c)
    yp, ys = x_prompt, x_sample
    st_ckv, st_kr, st_dk, st_dv = [], [], [], []
    for i in range(DEPTH):
        kind, j = i % N_MIXERS, i // N_MIXERS
        mod_p = jnp.split((cond_ctx @ ada_w[i] + ada_b[i])[:, None, :], N_MOD, axis=-1)
        mod_s = jnp.split((cond_lat @ ada_w[i] + ada_b[i])[:, None, :], N_MOD, axis=-1)
        hp = _modulate(yp, norm1_g[i], mod_p[0], mod_p[1])
        hs = _modulate(ys, norm1_g[i], mod_s[0], mod_s[1])
        if kind == 0:
            prm = (mla_w_down[j], mla_q_norm[j], mla_kv_norm[j], mla_w_uq[j], mla_w_uk[j], mla_w_uv[j],
                   mla_qn_nope[j], mla_qn_rope[j], mla_kn_nope[j], mla_kn_rope[j], mla_w_o[j])
            mp, ckv_p, kr_p = _mla(hp, None, None, *prm)
            ms, _, _ = _mla(hs, rope_mla, (cache_mla_ckv[:, j], cache_mla_krope[:, j]), *prm)
            st_ckv.append(ckv_p)
            st_kr.append(kr_p)
        elif kind == 1:
            lam_init = 0.8 - 0.6 * math.exp(-0.3 * i)
            prm = (diff_w_qkv[j], diff_qn[j], diff_kn[j], diff_lq1[j], diff_lk1[j], diff_lq2[j], diff_lk2[j],
                   diff_head_norm[j], diff_w_o[j])
            mp, k_p, v_p = _diff(hp, None, None, lam_init, *prm)
            ms, _, _ = _diff(hs, rope_diff, (cache_diff_k[:, j], cache_diff_v[:, j]), lam_init, *prm)
            st_dk.append(k_p)
            st_dv.append(v_p)
        elif kind == 2:
            prm = (sconv_w_in[j], sconv_w[j], sconv_w_out[j])
            mp = _short_conv(hp, *prm)
            ms = _short_conv(hs, *prm)
        else:
            prm = (gmlp_w_in[j], gmlp_v_norm[j], gmlp_w_s[j], gmlp_b_s[j], gmlp_w_out[j])
            mp = _chunk_gmlp(hp, *prm)
            ms = _chunk_gmlp(hs, *prm)
        yp = yp + mod_p[2] * mp
        ys = ys + mod_s[2] * ms
        fprm = (ffn_w_in[i], ffn_conv_w[i], ffn_conv_b[i], ffn_w_out[i])
        yp = yp + mod_p[5] * _conv_ffn(_modulate(yp, norm2_g[i], mod_p[3], mod_p[4]), *fprm)
        ys = ys + mod_s[5] * _conv_ffn(_modulate(ys, norm2_g[i], mod_s[3], mod_s[4]), *fprm)
    state_mla_ckv = jnp.stack(st_ckv, axis=1)
    state_mla_krope = jnp.stack(st_kr, axis=1)
    state_diff_k = jnp.stack(st_dk, axis=1)
    state_diff_v = jnp.stack(st_dv, axis=1)
    return (yp, ys, state_mla_ckv, state_mla_krope, state_diff_k, state_diff_v)
```

```python
import functools
import math

import jax
import jax.numpy as jnp
from jax import lax
from jax.experimental import pallas as pl
from jax.experimental.pallas import tpu as pltpu

D = 1024
BATCH, SEQ = 16, 256
DEC_BATCH, DEC_SEQ = 8, 2048
PAST = 256
DEPTH = 4
GRID_W = 64
EPS = 1e-6
ROPE_THETA = 10000.0
N_MOD = 6
MLA_HEADS, MLA_NOPE, MLA_ROPE, MLA_V = 8, 128, 64, 128
MLA_Q_LORA, MLA_KV_LORA = 768, 256
DIFF_HEADS, DIFF_HD = 8, 64
GMLP_CHUNK, GMLP_GROUPS = 128, 8
FFN_HIDDEN = 2816

T_P = BATCH * SEQ
T_S = DEC_BATCH * DEC_SEQ
T = T_P + T_S
TM = 256
NB_P = T_P // TM
NB = T // TM
BPS = DEC_SEQ // TM
KV_LEN = PAST + DEC_SEQ
T_KV = DEC_BATCH * KV_LEN + T_P
MOD_ROWS = 16
LANE = 128
MLA_DK = 256
HC = 256
HALO_BF16 = 16
HALO_F32 = 8
VMEM_LIMIT = 56 * 1024 * 1024

F32 = jnp.float32
BF16 = jnp.bfloat16


def _params(n_axes=1, vmem=VMEM_LIMIT):
    return pltpu.CompilerParams(dimension_semantics=("arbitrary",) * n_axes,
                                vmem_limit_bytes=vmem)


def _dot(a, b):
    return jnp.dot(a, b, preferred_element_type=F32)


def _dot_t(a, b):
    return lax.dot_general(a, b, (((1,), (1,)), ((), ())), preferred_element_type=F32)


def _rms(x, g):
    ms = jnp.mean(x * x, axis=-1, keepdims=True)
    return (x * lax.rsqrt(ms + EPS)) * g


def _modulate(x, g, shift, scale):
    return _rms(x, g) * (1.0 + scale) + shift


def _sigmoid(x):
    return 1.0 / (1.0 + jnp.exp(-x))


def _mod_parts(mod_ref):
    m = mod_ref[0]
    return [m[:, k * D:(k + 1) * D] for k in range(N_MOD)]


def _full(shape):
    return pl.BlockSpec(shape, lambda *_: (0,) * len(shape))


def _row_spec(width, rows=TM):
    return pl.BlockSpec((rows, width), lambda i: (i, 0))


def _mod_spec(layer):
    def imap(i):
        j = jnp.maximum(i - NB_P, 0)
        return (layer * MOD_ROWS + jnp.where(i < NB_P, 0, 1 + j // BPS), 0, 0)
    return pl.BlockSpec((1, 1, N_MOD * D), imap)


def _rope_spec():
    def imap(i):
        j = jnp.maximum(i - NB_P, 0)
        return (jnp.where(i < NB_P, BPS, j % BPS), 0)
    return pl.BlockSpec((TM, LANE), imap)


def _seq_edges(i):
    row0 = i * TM
    seq_len = jnp.where(row0 < T_P, SEQ, DEC_SEQ)
    pos = row0 % seq_len
    return pos != 0, pos + TM != seq_len


def _group_rms(xb, gain, left):
    sq = xb * xb
    sl = jnp.sum(jnp.where(left, sq, 0.0), axis=-1, keepdims=True)
    sr = jnp.sum(jnp.where(left, 0.0, sq), axis=-1, keepdims=True)
    ms = jnp.where(left, sl, sr) * (1.0 / 64.0)
    return (xb * lax.rsqrt(ms + EPS)) * gain


def _rope128(xb, cos, sin, first):
    swapped = jnp.where(first, pltpu.roll(xb, LANE - 32, 1), pltpu.roll(xb, 32, 1))
    return xb * cos + swapped * sin


ADA_TN = 1536


def _adaln_kernel(cond_ref, w_ref, b_ref, o_ref):
    c = cond_ref[...]
    a = (c * _sigmoid(c)).astype(BF16)
    o_ref[0] = _dot(a, w_ref[0].astype(BF16)) + b_ref[0]


def _adaln(cond, ada_w, ada_b):
    n = N_MOD * D
    return pl.pallas_call(
        _adaln_kernel,
        out_shape=jax.ShapeDtypeStruct((DEPTH, MOD_ROWS, n), F32),
        grid=(DEPTH, n // ADA_TN),
        in_specs=[pl.BlockSpec((MOD_ROWS, D), lambda l, j: (0, 0)),
                  pl.BlockSpec((1, D, ADA_TN), lambda l, j: (l, 0, j)),
                  pl.BlockSpec((1, 1, ADA_TN), lambda l, j: (l, 0, j))],
        out_specs=pl.BlockSpec((1, MOD_ROWS, ADA_TN), lambda l, j: (l, 0, j)),
        compiler_params=_params(2),
        name="adaln",
    )(cond, ada_w, ada_b.reshape(DEPTH, 1, n))


MLA_DOWN_W = MLA_Q_LORA + MLA_KV_LORA + LANE


def _mla_in_kernel(x_ref, mod_ref, g1_ref, wd_ref, qn_ref, kvn_ref, knr_ref, wuq_ref,
                   qnn_ref, qnr_ref, cos_ref, sin_ref, q_ref, ckv_ref, kr_ref):
    shift, scale = _mod_parts(mod_ref)[:2]
    h = _modulate(x_ref[...], g1_ref[...], shift, scale).astype(BF16)
    d = _dot(h, wd_ref[...])
    cq = _rms(d[:, :MLA_Q_LORA], qn_ref[...]).astype(BF16)
    ckv_ref[...] = _rms(d[:, MLA_Q_LORA:MLA_Q_LORA + MLA_KV_LORA], kvn_ref[...])

    lane = lax.broadcasted_iota(jnp.int32, (1, LANE), 1)
    first = (lane % 64) < 32
    cos, sin = cos_ref[...], sin_ref[...]

    def rope_part(xb, gain):
        ms = jnp.sum(xb * xb, axis=-1, keepdims=True) * (1.0 / MLA_ROPE)
        return _rope128((xb * lax.rsqrt(ms + EPS)) * gain, cos, sin, first)

    kr_ref[...] = rope_part(d[:, MLA_Q_LORA + MLA_KV_LORA:], knr_ref[...])

    q = _dot(cq, wuq_ref[...])
    for hh in range(MLA_HEADS):
        c0 = hh * MLA_DK
        q_ref[:, c0:c0 + LANE] = _rms(q[:, c0:c0 + LANE], qnn_ref[...]).astype(BF16)
        q_ref[:, c0 + LANE:c0 + MLA_DK] = rope_part(q[:, c0 + LANE:c0 + MLA_DK],
                                                    qnr_ref[...]).astype(BF16)


def _mla_in(x, mods, layer, g1, wd, qn, kvn, knr, wuq, qnn, qnr, cos_t, sin_t):
    return pl.pallas_call(
        _mla_in_kernel,
        out_shape=(jax.ShapeDtypeStruct((T, MLA_HEADS * MLA_DK), BF16),
                   jax.ShapeDtypeStruct((T, MLA_KV_LORA), F32),
                   jax.ShapeDtypeStruct((T, LANE), F32)),
        grid=(NB,),
        in_specs=[_row_spec(D), _mod_spec(layer), _full((1, D)), _full((D, MLA_DOWN_W)),
                  _full((1, MLA_Q_LORA)), _full((1, MLA_KV_LORA)), _full((1, LANE)),
                  _full((MLA_Q_LORA, MLA_HEADS * MLA_DK)), _full((1, LANE)), _full((1, LANE)),
                  _rope_spec(), _rope_spec()],
        out_specs=(_row_spec(MLA_HEADS * MLA_DK), _row_spec(MLA_KV_LORA), _row_spec(LANE)),
        compiler_params=_params(),
        name="mla_in",
    )(x, mods, g1, wd, qn, kvn, knr, wuq, qnn, qnr, cos_t, sin_t)


def _mla_kv_kernel(ckv_ref, kr_ref, wuk_ref, wuv_ref, knn_ref, k_ref, v_ref):
    c = ckv_ref[...].astype(BF16)
    kn = _dot(c, wuk_ref[...])
    v_ref[...] = _dot(c, wuv_ref[...]).astype(BF16)
    kr = kr_ref[...].astype(BF16)
    for hh in range(MLA_HEADS):
        c0 = hh * MLA_DK
        k_ref[:, c0:c0 + LANE] = _rms(kn[:, hh * LANE:(hh + 1) * LANE], knn_ref[...]).astype(BF16)
        k_ref[:, c0 + LANE:c0 + MLA_DK] = kr


def _mla_kv(ckv_all, kr_all, wuk, wuv, knn):
    return pl.pallas_call(
        _mla_kv_kernel,
        out_shape=(jax.ShapeDtypeStruct((T_KV, MLA_HEADS * MLA_DK), BF16),
                   jax.ShapeDtypeStruct((T_KV, MLA_HEADS * MLA_V), BF16)),
        grid=(T_KV // TM,),
        in_specs=[_row_spec(MLA_KV_LORA), _row_spec(LANE),
                  _full((MLA_KV_LORA, MLA_HEADS * MLA_NOPE)), _full((MLA_KV_LORA, MLA_HEADS * MLA_V)),
                  _full((1, LANE))],
        out_specs=(_row_spec(MLA_HEADS * MLA_DK), _row_spec(MLA_HEADS * MLA_V)),
        compiler_params=_params(),
        name="mla_kv",
    )(ckv_all, kr_all, wuk, wuv, knn)


def _softmax_pv(s, v):
    m = jnp.max(s, axis=-1, keepdims=True)
    p = jnp.exp(s - m)
    l = jnp.sum(p, axis=-1, keepdims=True)
    return _dot(p.astype(BF16), v) * (1.0 / l)


def _mla_attn_kernel(q_ref, k_ref, v_ref, *rest):
    o_ref = rest[-1]
    scale = (MLA_NOPE + MLA_ROPE) ** -0.5
    for hh in range(MLA_HEADS):
        s = _dot_t(q_ref[:, hh * MLA_DK:(hh + 1) * MLA_DK],
                   k_ref[:, hh * MLA_DK:(hh + 1) * MLA_DK]) * scale
        o = _softmax_pv(s, v_ref[:, hh * MLA_V:(hh + 1) * MLA_V])
        o_ref[:, hh * MLA_V:(hh + 1) * MLA_V] = o.astype(BF16)


def _diff_attn_kernel(lam_init, q_ref, k_ref, v_ref, lq1_ref, lk1_ref, lq2_ref, lk2_ref,
                      hn_ref, *rest):
    o_ref = rest[-1]
    scale = DIFF_HD ** -0.5
    lam = (jnp.exp(jnp.sum(lq1_ref[...] * lk1_ref[...], axis=-1, keepdims=True))
           - jnp.exp(jnp.sum(lq2_ref[...] * lk2_ref[...], axis=-1, keepdims=True)) + lam_init)
    lane = lax.broadcasted_iota(jnp.int32, (1, LANE), 1)
    left = lane < DIFF_HD
    zero = jnp.zeros((), BF16)
    for hh in range(DIFF_HEADS):
        cs = slice(hh * LANE, (hh + 1) * LANE)
        qb, kb, vb = q_ref[:, cs], k_ref[:, cs], v_ref[:, cs]
        o0 = _softmax_pv(_dot_t(jnp.where(left, qb, zero), kb) * scale, vb)
        o1 = _softmax_pv(_dot_t(jnp.where(left, zero, qb), kb) * scale, vb)
        o = o0 - lam * o1
        o_ref[:, cs] = (_rms(o, hn_ref[...]) * (1.0 - lam_init)).astype(BF16)


def _attention(body, n_small, name, q, k, v, smalls, dk, dv):
    def call(batches, sk, q_off, k_off, prev):
        nq = (SEQ if prev is None else DEC_SEQ) // TM
        in_specs = [pl.BlockSpec((TM, dk), lambda b, i: (q_off + b * nq + i, 0)),
                    pl.BlockSpec((sk, dk), lambda b, i: (k_off + b, 0)),
                    pl.BlockSpec((sk, dv), lambda b, i: (k_off + b, 0))]
        in_specs += [_full(s.shape) for s in smalls]
        args = [q, k, v, *smalls]
        aliases = {}
        if prev is not None:
            in_specs.append(pl.BlockSpec(memory_space=pl.ANY))
            args.append(prev)
            aliases = {len(args) - 1: 0}
        return pl.pallas_call(
            body,
            out_shape=jax.ShapeDtypeStruct((T, dv), BF16),
            grid=(batches, nq),
            in_specs=in_specs,
            out_specs=pl.BlockSpec((TM, dv), lambda b, i: (q_off + b * nq + i, 0)),
            input_output_aliases=aliases,
            compiler_params=_params(2),
            name=name + ("_prompt" if prev is None else "_latent"),
        )(*args)

    o = call(BATCH, SEQ, 0, DEC_BATCH * KV_LEN // SEQ, None)
    return call(DEC_BATCH, KV_LEN, NB_P, 0, o)


def _residual_and_ffn_in(x, mix, mods, g2):
    _, _, gate1, shift2, scale2, _ = mods
    y = x + gate1 * mix
    return y, _modulate(y, g2, shift2, scale2).astype(BF16)


def _mix_out_kernel(x_ref, o_ref, mod_ref, wo_ref, g2_ref, y_ref, hf_ref):
    mix = _dot(o_ref[...], wo_ref[...])
    y, hf = _residual_and_ffn_in(x_ref[...], mix, _mod_parts(mod_ref), g2_ref[...])
    y_ref[...] = y
    hf_ref[...] = hf


def _mix_out(x, o, mods, layer, wo, g2):
    return pl.pallas_call(
        _mix_out_kernel,
        out_shape=(jax.ShapeDtypeStruct((T, D), F32), jax.ShapeDtypeStruct((T, D), BF16)),
        grid=(NB,),
        in_specs=[_row_spec(D), _row_spec(D), _mod_spec(layer), _full((D, D)), _full((1, D))],
        out_specs=(_row_spec(D), _row_spec(D)),
        compiler_params=_params(),
        name="mix_out",
    )(x, o, mods, wo, g2)


def _conv3(zs_ref, w, off):
    return (zs_ref[off - 1:off - 1 + TM, :] * w[0:1] + zs_ref[off:off + TM, :] * w[1:2]
            + zs_ref[off + 1:off + 1 + TM, :] * w[2:3])


def _ffn_kernel(y_ref, hf_ref, prev_ref, next_ref, mod_ref, win_ref, cw_ref, cb_ref, wout_ref,
                out_ref, lhs_ref, zg_ref, zu_ref):
    has_prev, has_next = _seq_edges(pl.program_id(0))
    hb = HALO_BF16
    zeros = jnp.zeros((hb, D), BF16)
    lhs_ref[0:hb, :] = jnp.where(has_prev, prev_ref[...], zeros)
    lhs_ref[hb:hb + TM, :] = hf_ref[...]
    lhs_ref[hb + TM:, :] = jnp.where(has_next, next_ref[...], zeros)
    lhs = lhs_ref[...]
    acc = jnp.zeros((TM, D), F32)
    for c in range(FFN_HIDDEN // HC):
        gs = slice(c * HC, (c + 1) * HC)
        us = slice(FFN_HIDDEN + c * HC, FFN_HIDDEN + (c + 1) * HC)
        zg_ref[...] = _dot(lhs, win_ref[:, gs])
        zu_ref[...] = _dot(lhs, win_ref[:, us])
        g = _conv3(zg_ref, cw_ref[:, gs], hb) + cb_ref[:, gs]
        u = _conv3(zu_ref, cw_ref[:, us], hb) + cb_ref[:, us]
        act = ((g * _sigmoid(g)) * u).astype(BF16)
        acc = acc + _dot(act, wout_ref[gs, :])
    gate2 = _mod_parts(mod_ref)[5]
    out_ref[...] = y_ref[...] + gate2 * acc


def _ffn(y, hf, mods, layer, win, cw, cb, wout):
    hb = HALO_BF16
    r = TM // hb
    return pl.pallas_call(
        _ffn_kernel,
        out_shape=jax.ShapeDtypeStruct((T, D), F32),
        grid=(NB,),
        in_specs=[_row_spec(D), _row_spec(D),
                  pl.BlockSpec((hb, D), lambda i: (jnp.maximum(i * r - 1, 0), 0)),
                  pl.BlockSpec((hb, D), lambda i: (jnp.minimum((i + 1) * r, T // hb - 1), 0)),
                  _mod_spec(layer), _full((D, 2 * FFN_HIDDEN)), _full((3, 2 * FFN_HIDDEN)),
                  _full((1, 2 * FFN_HIDDEN)), _full((FFN_HIDDEN, D))],
        out_specs=_row_spec(D),
        scratch_shapes=[pltpu.VMEM((TM + 2 * hb, D), BF16),
                        pltpu.VMEM((TM + 2 * hb, HC), F32),
                        pltpu.VMEM((TM + 2 * hb, HC), F32)],
        compiler_params=_params(),
        name="ffn",
    )(y, hf, hf, hf, mods, win, cw, cb, wout)


def _diff_in_kernel(x_ref, mod_ref, g1_ref, w_ref, qn_ref, kn_ref, cos_ref, sin_ref,
                    q_ref, k_ref, v_ref, kst_ref, vst_ref):
    i = pl.program_id(0)
    shift, scale = _mod_parts(mod_ref)[:2]
    h = _modulate(x_ref[...], g1_ref[...], shift, scale).astype(BF16)
    z = _dot(h, w_ref[...])
    lane = lax.broadcasted_iota(jnp.int32, (1, LANE), 1)
    left = lane < DIFF_HD
    first = (lane % 64) < 32
    cos, sin = cos_ref[...], sin_ref[...]
    v = z[:, 2 * D:]
    v_ref[...] = v.astype(BF16)

    @pl.when(i < NB_P)
    def _():
        vst_ref[...] = v

    for hh in range(DIFF_HEADS):
        cs = slice(hh * LANE, (hh + 1) * LANE)
        qn = _group_rms(z[:, hh * LANE:(hh + 1) * LANE], qn_ref[...], left)
        q_ref[:, cs] = _rope128(qn, cos, sin, first).astype(BF16)
        kn = _group_rms(z[:, D + hh * LANE:D + (hh + 1) * LANE], kn_ref[...], left)
        k_ref[:, cs] = _rope128(kn, cos, sin, first).astype(BF16)

        @pl.when(i < NB_P)
        def _():
            kst_ref[:, cs] = kn


def _diff_in(x, mods, layer, g1, w, qn, kn, cos_t, sin_t):
    st_spec = pl.BlockSpec((TM, D), lambda i: (jnp.minimum(i, NB_P - 1), 0))
    return pl.pallas_call(
        _diff_in_kernel,
        out_shape=(jax.ShapeDtypeStruct((T, D), BF16), jax.ShapeDtypeStruct((T, D), BF16),
                   jax.ShapeDtypeStruct((T, D), BF16), jax.ShapeDtypeStruct((T_P, D), F32),
                   jax.ShapeDtypeStruct((T_P, D), F32)),
        grid=(NB,),
        in_specs=[_row_spec(D), _mod_spec(layer), _full((1, D)), _full((D, 3 * D)),
                  _full((1, LANE)), _full((1, LANE)), _rope_spec(), _rope_spec()],
        out_specs=(_row_spec(D), _row_spec(D), _row_spec(D), st_spec, st_spec),
        compiler_params=_params(),
        name="diff_in",
    )(x, mods, g1, w, qn, kn, cos_t, sin_t)


def _sconv_kernel(x_ref, prev_ref, next_ref, mod_ref, g1_ref, win_ref, cw_ref, wout_ref, g2_ref,
                  y_ref, hf_ref, p_ref):
    has_prev, has_next = _seq_edges(pl.program_id(0))
    hb = HALO_F32
    mods = _mod_parts(mod_ref)
    x = x_ref[...]
    mod_in = lambda a: _modulate(a, g1_ref[...], mods[0], mods[1])
    zeros = jnp.zeros((hb, D), F32)
    lhs = jnp.concatenate([jnp.where(has_prev, mod_in(prev_ref[...]), zeros), mod_in(x),
                           jnp.where(has_next, mod_in(next_ref[...]), zeros)],
                          axis=0).astype(BF16)
    acc = jnp.zeros((TM, D), F32)
    for c in range(D // HC):
        cs = slice(c * HC, (c + 1) * HC)
        gb = _dot(lhs, win_ref[:, c * HC:(c + 1) * HC])
        gc = _dot(lhs, win_ref[:, D + c * HC:D + (c + 1) * HC])
        u = _dot(lhs, win_ref[:, 2 * D + c * HC:2 * D + (c + 1) * HC])
        p_ref[...] = gc * u
        mixed = gb[hb:hb + TM] * _conv3(p_ref, cw_ref[:, cs], hb)
        acc = acc + _dot(mixed.astype(BF16), wout_ref[cs, :])
    y, hf = _residual_and_ffn_in(x, acc, mods, g2_ref[...])
    y_ref[...] = y
    hf_ref[...] = hf


def _sconv(x, mods, layer, g1, win, cw, wout, g2):
    hb = HALO_F32
    r = TM // hb
    return pl.pallas_call(
        _sconv_kernel,
        out_shape=(jax.ShapeDtypeStruct((T, D), F32), jax.ShapeDtypeStruct((T, D), BF16)),
        grid=(NB,),
        in_specs=[_row_spec(D),
                  pl.BlockSpec((hb, D), lambda i: (jnp.maximum(i * r - 1, 0), 0)),
                  pl.BlockSpec((hb, D), lambda i: (jnp.minimum((i + 1) * r, T // hb - 1), 0)),
                  _mod_spec(layer), _full((1, D)), _full((D, 3 * D)), _full((3, D)),
                  _full((D, D)), _full((1, D))],
        out_specs=(_row_spec(D), _row_spec(D)),
        scratch_shapes=[pltpu.VMEM((TM + 2 * hb, HC), F32)],
        compiler_params=_params(),
        name="sconv",
    )(x, x, x, mods, g1, win, cw, wout, g2)


def _gmlp_kernel(x_ref, mod_ref, g1_ref, win_ref, vn_ref, ws_ref, bs_ref, wout_ref, g2_ref,
                 y_ref, hf_ref, gated_ref):
    mods = _mod_parts(mod_ref)
    x = x_ref[...]
    h = _modulate(x, g1_ref[...], mods[0], mods[1]).astype(BF16)
    z = _dot(h, win_ref[...])
    z = z * (0.5 * (1.0 + jnp.tanh(math.sqrt(2.0 / math.pi) * (z + 0.044715 * (z * z * z)))))
    u = z[:, :D]
    v = _rms(z[:, D:], vn_ref[...]).astype(BF16)
    bs = bs_ref[...]
    for g in range(GMLP_GROUPS):
        cs = slice(g * LANE, (g + 1) * LANE)
        w = ws_ref[g]
        for r in range(TM // GMLP_CHUNK):
            rs = slice(r * GMLP_CHUNK, (r + 1) * GMLP_CHUNK)
            mixed = _dot(w, v[rs, cs]) + bs[:, g:g + 1]
            gated_ref[rs, cs] = (u[rs, cs] * mixed).astype(BF16)
    y, hf = _residual_and_ffn_in(x, _dot(gated_ref[...], wout_ref[...]), mods, g2_ref[...])
    y_ref[...] = y
    hf_ref[...] = hf


def _gmlp(x, mods, layer, g1, win, vn, ws, bs_t, wout, g2):
    return pl.pallas_call(
        _gmlp_kernel,
        out_shape=(jax.ShapeDtypeStruct((T, D), F32), jax.ShapeDtypeStruct((T, D), BF16)),
        grid=(NB,),
        in_specs=[_row_spec(D), _mod_spec(layer), _full((1, D)), _full((D, 2 * D)), _full((1, D)),
                  _full((GMLP_GROUPS, GMLP_CHUNK, GMLP_CHUNK)), _full((GMLP_CHUNK, GMLP_GROUPS)),
                  _full((D, D)), _full((1, D))],
        out_specs=(_row_spec(D), _row_spec(D)),
        scratch_shapes=[pltpu.VMEM((TM, D), BF16)],
        compiler_params=_params(),
        name="gmlp",
    )(x, mods, g1, win, vn, ws, bs_t, wout, g2)


def _rope_tables():
    rows = DEC_SEQ // GRID_W
    row = jnp.repeat(jnp.arange(rows, dtype=F32), GRID_W)
    col = jnp.tile(jnp.arange(GRID_W, dtype=F32), rows)
    n_freq = MLA_ROPE // 4
    inv_freq = ROPE_THETA ** (-jnp.arange(n_freq, dtype=F32) / n_freq)
    ang = jnp.concatenate([row[:, None] * inv_freq, col[:, None] * inv_freq], axis=-1)
    cos, sin = jnp.cos(ang), jnp.sin(ang)
    cos = jnp.concatenate([cos, jnp.ones((TM, cos.shape[1]), F32)], axis=0)
    sin = jnp.concatenate([sin, jnp.zeros((TM, sin.shape[1]), F32)], axis=0)
    return jnp.tile(cos, (1, 4)), jnp.tile(jnp.concatenate([-sin, sin], axis=1), (1, 2))


def _pad_lanes(a, width=LANE):
    return jnp.pad(a, [(0, 0)] * (a.ndim - 1) + [(0, width - a.shape[-1])])


def _kv_rows(cache, new):
    w = new.shape[-1]
    lat = jnp.concatenate([cache, new[T_P:].reshape(DEC_BATCH, DEC_SEQ, w)], axis=1)
    return jnp.concatenate([lat.reshape(DEC_BATCH * KV_LEN, w), new[:T_P]], axis=0)


def _heads_first(a):
    return jnp.swapaxes(a, -3, -2).reshape(a.shape[:-3] + (D,))


def kernel(x_prompt, x_sample, cache_mla_ckv, cache_mla_krope, cache_diff_k, cache_diff_v, c, c_ctx, ada_w, ada_b, norm1_g, norm2_g, mla_w_down, mla_q_norm, mla_kv_norm, mla_w_uq, mla_w_uk, mla_w_uv, mla_qn_nope, mla_qn_rope, mla_kn_nope, mla_kn_rope, mla_w_o, diff_w_qkv, diff_qn, diff_kn, diff_lq1, diff_lk1, diff_lq2, diff_lk2, diff_head_norm, diff_w_o, sconv_w_in, sconv_w, sconv_w_out, gmlp_w_in, gmlp_v_norm, gmlp_w_s, gmlp_b_s, gmlp_w_out, ffn_w_in, ffn_conv_w, ffn_conv_b, ffn_w_out):
    x = jnp.concatenate([x_prompt.reshape(T_P, D), x_sample.reshape(T_S, D)], axis=0)
    cond = jnp.concatenate([c_ctx[None], c, jnp.zeros((MOD_ROWS - 1 - DEC_BATCH, D), F32)], axis=0)
    mods = _adaln(cond, ada_w, ada_b).reshape(DEPTH * MOD_ROWS, 1, N_MOD * D)
    cos_t, sin_t = _rope_tables()
    row = lambda a: a.reshape(1, -1)

    def ffn(y, hf, i):
        return _ffn(y, hf, mods, i, ffn_w_in[i].astype(BF16), ffn_conv_w[i], row(ffn_conv_b[i]),
                    ffn_w_out[i].astype(BF16))

    wd = _pad_lanes(mla_w_down[0], MLA_DOWN_W).astype(BF16)
    wuq = _pad_lanes(mla_w_uq[0].reshape(MLA_Q_LORA, MLA_HEADS, MLA_NOPE + MLA_ROPE), MLA_DK)
    wuq = wuq.reshape(MLA_Q_LORA, MLA_HEADS * MLA_DK).astype(BF16)
    q, ckv, kr = _mla_in(x, mods, 0, row(norm1_g[0]), wd, row(mla_q_norm[0]), row(mla_kv_norm[0]),
                         _pad_lanes(row(mla_kn_rope[0])), wuq, row(mla_qn_nope[0]),
                         _pad_lanes(row(mla_qn_rope[0])), cos_t, sin_t)
    state_ckv = ckv[:T_P].reshape(BATCH, 1, SEQ, MLA_KV_LORA)
    state_kr = kr[:T_P, :MLA_ROPE].reshape(BATCH, 1, SEQ, MLA_ROPE)
    k, v = _mla_kv(_kv_rows(cache_mla_ckv[:, 0], ckv), _kv_rows(_pad_lanes(cache_mla_krope[:, 0]), kr),
                   mla_w_uk[0].astype(BF16), mla_w_uv[0].astype(BF16), row(mla_kn_nope[0]))
    o = _attention(_mla_attn_kernel, 0, "mla_attn", q, k, v, [], MLA_HEADS * MLA_DK, MLA_HEADS * MLA_V)
    y, hf = _mix_out(x, o, mods, 0, mla_w_o[0].astype(BF16), row(norm2_g[0]))
    x = ffn(y, hf, 0)

    lam_init = 0.8 - 0.6 * math.exp(-0.3 * 1)
    wq, wk, wv = jnp.split(diff_w_qkv[0], 3, axis=-1)
    perm = lambda w: _heads_first(w.reshape(D, 2, DIFF_HEADS, DIFF_HD))
    w_qkv = jnp.concatenate([perm(wq), perm(wk), wv], axis=-1).astype(BF16)
    pair = lambda g: row(jnp.tile(g, 2))
    q, k, v, kst, vst = _diff_in(x, mods, 1, row(norm1_g[1]), w_qkv, pair(diff_qn[0]), pair(diff_kn[0]),
                                 cos_t, sin_t)
    state_dk = jnp.swapaxes(kst.reshape(BATCH, SEQ, DIFF_HEADS, 2, DIFF_HD), 2, 3)[:, None]
    state_dv = vst.reshape(BATCH, 1, SEQ, DIFF_HEADS, 2 * DIFF_HD)
    k_all = _kv_rows(_heads_first(cache_diff_k[:, 0]).astype(BF16), k)
    v_all = _kv_rows(cache_diff_v[:, 0].reshape(DEC_BATCH, PAST, D).astype(BF16), v)
    smalls = [row(diff_lq1[0]), row(diff_lk1[0]), row(diff_lq2[0]), row(diff_lk2[0]),
              row(diff_head_norm[0])]
    o = _attention(functools.partial(_diff_attn_kernel, lam_init), 5, "diff_attn", q, k_all, v_all,
                   smalls, D, D)
    y, hf = _mix_out(x, o, mods, 1, diff_w_o[0].astype(BF16), row(norm2_g[1]))
    x = ffn(y, hf, 1)

    y, hf = _sconv(x, mods, 2, row(norm1_g[2]), sconv_w_in[0].astype(BF16), sconv_w[0],
                   sconv_w_out[0].astype(BF16), row(norm2_g[2]))
    x = ffn(y, hf, 2)

    y, hf = _gmlp(x, mods, 3, row(norm1_g[3]), gmlp_w_in[0].astype(BF16), row(gmlp_v_norm[0]),
                  gmlp_w_s[0].astype(BF16), gmlp_b_s[0].T, gmlp_w_out[0].astype(BF16), row(norm2_g[3]))
    x = ffn(y, hf, 3)

    return (x[:T_P].reshape(BATCH, SEQ, D), x[T_P:].reshape(DEC_BATCH, DEC_SEQ, D),
            state_ckv, state_kr, state_dk, state_dv)
```

```python
import functools
import math

import jax
import jax.numpy as jnp
from jax import lax
from jax.experimental import pallas as pl
from jax.experimental.pallas import tpu as pltpu

D = 1024
BATCH, SEQ = 16, 256
DEC_BATCH, DEC_SEQ = 8, 2048
PAST = 256
DEPTH = 4
GRID_W = 64
EPS = 1e-6
ROPE_THETA = 10000.0
N_MOD = 6
MLA_HEADS, MLA_NOPE, MLA_ROPE, MLA_V = 8, 128, 64, 128
MLA_Q_LORA, MLA_KV_LORA = 768, 256
DIFF_HEADS, DIFF_HD = 8, 64
GMLP_CHUNK, GMLP_GROUPS = 128, 8
FFN_HIDDEN = 2816

T_P = BATCH * SEQ
T_S = DEC_BATCH * DEC_SEQ
T = T_P + T_S
TM = 256
NB_P = T_P // TM
NB = T // TM
BPS = DEC_SEQ // TM
KV_LEN = PAST + DEC_SEQ
T_KV = DEC_BATCH * KV_LEN + T_P
MOD_ROWS = 16
LANE = 128
MLA_DK = 256
HC = 256
HALO_BF16 = 16
HALO_F32 = 8
VMEM_LIMIT = 56 * 1024 * 1024

F32 = jnp.float32
BF16 = jnp.bfloat16


def _params(n_axes=1, vmem=VMEM_LIMIT):
    return pltpu.CompilerParams(dimension_semantics=("arbitrary",) * n_axes,
                                vmem_limit_bytes=vmem)


def _dot(a, b):
    return jnp.dot(a, b, preferred_element_type=F32)


def _rms(x, g):
    ms = jnp.mean(x * x, axis=-1, keepdims=True)
    return (x * lax.rsqrt(ms + EPS)) * g


def _modulate(x, g, shift, scale):
    return _rms(x, g) * (1.0 + scale) + shift


def _sigmoid(x):
    return 1.0 / (1.0 + jnp.exp(-x))


def _mod_parts(mod_ref):
    m = mod_ref[0]
    return [m[:, k * D:(k + 1) * D] for k in range(N_MOD)]


def _full(shape):
    return pl.BlockSpec(shape, lambda *_: (0,) * len(shape))


def _row_spec(width, rows=TM):
    return pl.BlockSpec((rows, width), lambda i: (i, 0))


def _mod_spec(layer):
    def imap(i):
        j = jnp.maximum(i - NB_P, 0)
        return (layer * MOD_ROWS + jnp.where(i < NB_P, 0, 1 + j // BPS), 0, 0)
    return pl.BlockSpec((1, 1, N_MOD * D), imap)


def _rope_spec():
    def imap(i):
        j = jnp.maximum(i - NB_P, 0)
        return (jnp.where(i < NB_P, BPS, j % BPS), 0)
    return pl.BlockSpec((TM, LANE), imap)


def _seq_edges(i):
    row0 = i * TM
    seq_len = jnp.where(row0 < T_P, SEQ, DEC_SEQ)
    pos = row0 % seq_len
    return pos != 0, pos + TM != seq_len


def _group_rms(xb, gain, left):
    sq = xb * xb
    sl = jnp.sum(jnp.where(left, sq, 0.0), axis=-1, keepdims=True)
    sr = jnp.sum(jnp.where(left, 0.0, sq), axis=-1, keepdims=True)
    ms = jnp.where(left, sl, sr) * (1.0 / 64.0)
    return (xb * lax.rsqrt(ms + EPS)) * gain


def _rope128(xb, cos, sin, first):
    swapped = jnp.where(first, pltpu.roll(xb, LANE - 32, 1), pltpu.roll(xb, 32, 1))
    return xb * cos + swapped * sin


ADA_TN = 1536


def _adaln_kernel(cond_ref, w_ref, b_ref, o_ref):
    c = cond_ref[...]
    a = (c * _sigmoid(c)).astype(BF16)
    o_ref[0] = _dot(a, w_ref[0].astype(BF16)) + b_ref[0]


def _adaln(cond, ada_w, ada_b):
    n = N_MOD * D
    return pl.pallas_call(
        _adaln_kernel,
        out_shape=jax.ShapeDtypeStruct((DEPTH, MOD_ROWS, n), F32),
        grid=(DEPTH, n // ADA_TN),
        in_specs=[pl.BlockSpec((MOD_ROWS, D), lambda l, j: (0, 0)),
                  pl.BlockSpec((1, D, ADA_TN), lambda l, j: (l, 0, j)),
                  pl.BlockSpec((1, 1, ADA_TN), lambda l, j: (l, 0, j))],
        out_specs=pl.BlockSpec((1, MOD_ROWS, ADA_TN), lambda l, j: (l, 0, j)),
        compiler_params=_params(2),
        name="adaln",
    )(cond, ada_w, ada_b.reshape(DEPTH, 1, n))


MLA_DOWN_W = MLA_Q_LORA + MLA_KV_LORA + LANE


def _mla_in_kernel(x_ref, mod_ref, g1_ref, wd_ref, qn_ref, kvn_ref, knr_ref, wuq_ref,
                   qnn_ref, qnr_ref, cos_ref, sin_ref, q_ref, ckv_ref, kr_ref):
    shift, scale = _mod_parts(mod_ref)[:2]
    h = _modulate(x_ref[...], g1_ref[...], shift, scale).astype(BF16)
    d = _dot(h, wd_ref[...])
    cq = _rms(d[:, :MLA_Q_LORA], qn_ref[...]).astype(BF16)
    ckv_ref[...] = _rms(d[:, MLA_Q_LORA:MLA_Q_LORA + MLA_KV_LORA], kvn_ref[...])

    lane = lax.broadcasted_iota(jnp.int32, (1, LANE), 1)
    first = (lane % 64) < 32
    cos, sin = cos_ref[...], sin_ref[...]

    def rope_part(xb, gain):
        ms = jnp.sum(xb * xb, axis=-1, keepdims=True) * (1.0 / MLA_ROPE)
        return _rope128((xb * lax.rsqrt(ms + EPS)) * gain, cos, sin, first)

    kr_ref[...] = rope_part(d[:, MLA_Q_LORA + MLA_KV_LORA:], knr_ref[...])

    q = _dot(cq, wuq_ref[...])
    for hh in range(MLA_HEADS):
        c0 = hh * MLA_DK
        q_ref[:, c0:c0 + LANE] = _rms(q[:, c0:c0 + LANE], qnn_ref[...]).astype(BF16)
        q_ref[:, c0 + LANE:c0 + MLA_DK] = rope_part(q[:, c0 + LANE:c0 + MLA_DK],
                                                    qnr_ref[...]).astype(BF16)


def _mla_in(x, mods, layer, g1, wd, qn, kvn, knr, wuq, qnn, qnr, cos_t, sin_t):
    return pl.pallas_call(
        _mla_in_kernel,
        out_shape=(jax.ShapeDtypeStruct((T, MLA_HEADS * MLA_DK), BF16),
                   jax.ShapeDtypeStruct((T, MLA_KV_LORA), F32),
                   jax.ShapeDtypeStruct((T, LANE), F32)),
        grid=(NB,),
        in_specs=[_row_spec(D), _mod_spec(layer), _full((1, D)), _full((D, MLA_DOWN_W)),
                  _full((1, MLA_Q_LORA)), _full((1, MLA_KV_LORA)), _full((1, LANE)),
                  _full((MLA_Q_LORA, MLA_HEADS * MLA_DK)), _full((1, LANE)), _full((1, LANE)),
                  _rope_spec(), _rope_spec()],
        out_specs=(_row_spec(MLA_HEADS * MLA_DK), _row_spec(MLA_KV_LORA), _row_spec(LANE)),
        compiler_params=_params(),
        name="mla_in",
    )(x, mods, g1, wd, qn, kvn, knr, wuq, qnn, qnr, cos_t, sin_t)


def _mla_kv_kernel(ckv_ref, kr_ref, wuk_ref, wuv_ref, knn_ref, k_ref, v_ref):
    c = ckv_ref[...].astype(BF16)
    kn = _dot(c, wuk_ref[...])
    v_ref[...] = _dot(c, wuv_ref[...]).astype(BF16)
    kr = kr_ref[...].astype(BF16)
    for hh in range(MLA_HEADS):
        c0 = hh * MLA_DK
        k_ref[:, c0:c0 + LANE] = _rms(kn[:, hh * LANE:(hh + 1) * LANE], knn_ref[...]).astype(BF16)
        k_ref[:, c0 + LANE:c0 + MLA_DK] = kr


def _mla_kv(ckv_all, kr_all, wuk, wuv, knn):
    return pl.pallas_call(
        _mla_kv_kernel,
        out_shape=(jax.ShapeDtypeStruct((T_KV, MLA_HEADS * MLA_DK), BF16),
                   jax.ShapeDtypeStruct((T_KV, MLA_HEADS * MLA_V), BF16)),
        grid=(T_KV // TM,),
        in_specs=[_row_spec(MLA_KV_LORA), _row_spec(LANE),
                  _full((MLA_KV_LORA, MLA_HEADS * MLA_NOPE)), _full((MLA_KV_LORA, MLA_HEADS * MLA_V)),
                  _full((1, LANE))],
        out_specs=(_row_spec(MLA_HEADS * MLA_DK), _row_spec(MLA_HEADS * MLA_V)),
        compiler_params=_params(),
        name="mla_kv",
    )(ckv_all, kr_all, wuk, wuv, knn)


LOG2E = math.log2(math.e)


def _softmax_pv(s2, v_ones):
    p = jnp.exp2(s2 - jnp.max(s2, axis=-1, keepdims=True))
    o = _dot(p.astype(BF16), v_ones)
    return o[:, :LANE] * (1.0 / o[:, LANE:LANE + 1])


def _with_ones(v):
    return jnp.concatenate([v, jnp.ones_like(v)], axis=1)


def _mla_attn_kernel(q_ref, kt_ref, v_ref, *rest):
    o_ref = rest[-1]
    scale2 = (MLA_NOPE + MLA_ROPE) ** -0.5 * LOG2E
    for hh in range(MLA_HEADS):
        ks = slice(hh * MLA_DK, (hh + 1) * MLA_DK)
        vs = slice(hh * MLA_V, (hh + 1) * MLA_V)
        o = _softmax_pv(_dot(q_ref[:, ks], kt_ref[0, ks, :]) * scale2, _with_ones(v_ref[:, vs]))
        o_ref[:, vs] = o.astype(BF16)


def _diff_attn_kernel(lam_init, q_ref, kt_ref, v_ref, lq1_ref, lk1_ref, lq2_ref, lk2_ref,
                      hn_ref, *rest):
    o_ref = rest[-1]
    lam = (jnp.exp(jnp.sum(lq1_ref[...] * lk1_ref[...], axis=-1, keepdims=True))
           - jnp.exp(jnp.sum(lq2_ref[...] * lk2_ref[...], axis=-1, keepdims=True)) + lam_init)
    lane = lax.broadcasted_iota(jnp.int32, (1, LANE), 1)
    left = lane < DIFF_HD
    zero = jnp.zeros((), BF16)
    for hh in range(DIFF_HEADS):
        cs = slice(hh * LANE, (hh + 1) * LANE)
        qb, kt, v1 = q_ref[:, cs], kt_ref[0, cs, :], _with_ones(v_ref[:, cs])
        o0 = _softmax_pv(_dot(jnp.where(left, qb, zero), kt) * LOG2E, v1)
        o1 = _softmax_pv(_dot(jnp.where(left, zero, qb), kt) * LOG2E, v1)
        o = o0 - lam * o1
        o_ref[:, cs] = (_rms(o, hn_ref[...]) * (1.0 - lam_init)).astype(BF16)


def _attention(body, name, q, k, v, smalls, dk, dv):
    n_lat = DEC_BATCH * KV_LEN
    kt_lat = jnp.swapaxes(k[:n_lat].reshape(DEC_BATCH, KV_LEN, dk), 1, 2)
    kt_prompt = jnp.swapaxes(k[n_lat:].reshape(BATCH, SEQ, dk), 1, 2)

    def call(batches, sk, q_off, v_off, kt, prev):
        nq = (SEQ if prev is None else DEC_SEQ) // TM
        in_specs = [pl.BlockSpec((TM, dk), lambda b, i: (q_off + b * nq + i, 0)),
                    pl.BlockSpec((1, dk, sk), lambda b, i: (b, 0, 0)),
                    pl.BlockSpec((sk, dv), lambda b, i: (v_off + b, 0))]
        in_specs += [_full(s.shape) for s in smalls]
        args = [q, kt, v, *smalls]
        aliases = {}
        if prev is not None:
            in_specs.append(pl.BlockSpec(memory_space=pl.ANY))
            args.append(prev)
            aliases = {len(args) - 1: 0}
        return pl.pallas_call(
            body,
            out_shape=jax.ShapeDtypeStruct((T, dv), BF16),
            grid=(batches, nq),
            in_specs=in_specs,
            out_specs=pl.BlockSpec((TM, dv), lambda b, i: (q_off + b * nq + i, 0)),
            input_output_aliases=aliases,
            compiler_params=_params(2),
            name=name + ("_prompt" if prev is None else "_latent"),
        )(*args)

    o = call(BATCH, SEQ, 0, n_lat // SEQ, kt_prompt, None)
    return call(DEC_BATCH, KV_LEN, NB_P, 0, kt_lat, o)


def _residual_and_ffn_in(x, mix, mods, g2):
    _, _, gate1, shift2, scale2, _ = mods
    y = x + gate1 * mix
    return y, _modulate(y, g2, shift2, scale2).astype(BF16)


def _mix_out_kernel(x_ref, o_ref, mod_ref, wo_ref, g2_ref, y_ref, hf_ref):
    mix = _dot(o_ref[...], wo_ref[...])
    y, hf = _residual_and_ffn_in(x_ref[...], mix, _mod_parts(mod_ref), g2_ref[...])
    y_ref[...] = y
    hf_ref[...] = hf


def _mix_out(x, o, mods, layer, wo, g2):
    return pl.pallas_call(
        _mix_out_kernel,
        out_shape=(jax.ShapeDtypeStruct((T, D), F32), jax.ShapeDtypeStruct((T, D), BF16)),
        grid=(NB,),
        in_specs=[_row_spec(D), _row_spec(D), _mod_spec(layer), _full((D, D)), _full((1, D))],
        out_specs=(_row_spec(D), _row_spec(D)),
        compiler_params=_params(),
        name="mix_out",
    )(x, o, mods, wo, g2)


def _conv3(zs_ref, w, off):
    return (zs_ref[off - 1:off - 1 + TM, :] * w[0:1] + zs_ref[off:off + TM, :] * w[1:2]
            + zs_ref[off + 1:off + 1 + TM, :] * w[2:3])


def _ffn_kernel(y_ref, hf_ref, prev_ref, next_ref, mod_ref, win_ref, cw_ref, cb_ref, wout_ref,
                out_ref, lhs_ref, z_ref, act_ref):
    has_prev, has_next = _seq_edges(pl.program_id(0))
    hb = HALO_BF16
    zeros = jnp.zeros((hb, D), BF16)
    lhs_ref[0:hb, :] = jnp.where(has_prev, prev_ref[...], zeros)
    lhs_ref[hb:hb + TM, :] = hf_ref[...]
    lhs_ref[hb + TM:, :] = jnp.where(has_next, next_ref[...], zeros)
    lhs = lhs_ref[...]
    for c in range(FFN_HIDDEN // HC):
        gs = slice(c * HC, (c + 1) * HC)
        us = slice(FFN_HIDDEN + c * HC, FFN_HIDDEN + (c + 1) * HC)
        z_ref[:, gs] = _dot(lhs, win_ref[:, gs])
        z_ref[:, us] = _dot(lhs, win_ref[:, us])
        g = _conv3(z_ref.at[:, gs], cw_ref[:, gs], hb) + cb_ref[:, gs]
        u = _conv3(z_ref.at[:, us], cw_ref[:, us], hb) + cb_ref[:, us]
        act_ref[:, gs] = ((g * _sigmoid(g)) * u).astype(BF16)
    gate2 = _mod_parts(mod_ref)[5]
    out_ref[...] = y_ref[...] + gate2 * _dot(act_ref[...], wout_ref[...])


def _ffn(y, hf, mods, layer, win, cw, cb, wout):
    hb = HALO_BF16
    r = TM // hb
    return pl.pallas_call(
        _ffn_kernel,
        out_shape=jax.ShapeDtypeStruct((T, D), F32),
        grid=(NB,),
        in_specs=[_row_spec(D), _row_spec(D),
                  pl.BlockSpec((hb, D), lambda i: (jnp.maximum(i * r - 1, 0), 0)),
                  pl.BlockSpec((hb, D), lambda i: (jnp.minimum((i + 1) * r, T // hb - 1), 0)),
                  _mod_spec(layer), _full((D, 2 * FFN_HIDDEN)), _full((3, 2 * FFN_HIDDEN)),
                  _full((1, 2 * FFN_HIDDEN)), _full((FFN_HIDDEN, D))],
        out_specs=_row_spec(D),
        scratch_shapes=[pltpu.VMEM((TM + 2 * hb, D), BF16),
                        pltpu.VMEM((TM + 2 * hb, 2 * FFN_HIDDEN), F32),
                        pltpu.VMEM((TM, FFN_HIDDEN), BF16)],
        compiler_params=_params(),
        name="ffn",
    )(y, hf, hf, hf, mods, win, cw, cb, wout)


def _diff_in_kernel(x_ref, mod_ref, g1_ref, w_ref, qn_ref, kn_ref, cos_ref, sin_ref,
                    q_ref, k_ref, v_ref, kst_ref, vst_ref):
    i = pl.program_id(0)
    shift, scale = _mod_parts(mod_ref)[:2]
    h = _modulate(x_ref[...], g1_ref[...], shift, scale).astype(BF16)
    z = _dot(h, w_ref[...])
    lane = lax.broadcasted_iota(jnp.int32, (1, LANE), 1)
    left = lane < DIFF_HD
    first = (lane % 64) < 32
    cos, sin = cos_ref[...], sin_ref[...]
    v = z[:, 2 * D:]
    v_ref[...] = v.astype(BF16)

    @pl.when(i < NB_P)
    def _():
        vst_ref[...] = v

    for hh in range(DIFF_HEADS):
        cs = slice(hh * LANE, (hh + 1) * LANE)
        qn = _group_rms(z[:, hh * LANE:(hh + 1) * LANE], qn_ref[...], left)
        q_ref[:, cs] = (_rope128(qn, cos, sin, first) * DIFF_HD ** -0.5).astype(BF16)
        kn = _group_rms(z[:, D + hh * LANE:D + (hh + 1) * LANE], kn_ref[...], left)
        k_ref[:, cs] = _rope128(kn, cos, sin, first).astype(BF16)

        @pl.when(i < NB_P)
        def _():
            kst_ref[:, cs] = kn


def _diff_in(x, mods, layer, g1, w, qn, kn, cos_t, sin_t):
    st_spec = pl.BlockSpec((TM, D), lambda i: (jnp.minimum(i, NB_P - 1), 0))
    return pl.pallas_call(
        _diff_in_kernel,
        out_shape=(jax.ShapeDtypeStruct((T, D), BF16), jax.ShapeDtypeStruct((T, D), BF16),
                   jax.ShapeDtypeStruct((T, D), BF16), jax.ShapeDtypeStruct((T_P, D), F32),
                   jax.ShapeDtypeStruct((T_P, D), F32)),
        grid=(NB,),
        in_specs=[_row_spec(D), _mod_spec(layer), _full((1, D)), _full((D, 3 * D)),
                  _full((1, LANE)), _full((1, LANE)), _rope_spec(), _rope_spec()],
        out_specs=(_row_spec(D), _row_spec(D), _row_spec(D), st_spec, st_spec),
        compiler_params=_params(),
        name="diff_in",
    )(x, mods, g1, w, qn, kn, cos_t, sin_t)


def _sconv_kernel(x_ref, prev_ref, next_ref, mod_ref, g1_ref, win_ref, cw_ref, wout_ref, g2_ref,
                  y_ref, hf_ref, p_ref):
    has_prev, has_next = _seq_edges(pl.program_id(0))
    hb = HALO_F32
    mods = _mod_parts(mod_ref)
    x = x_ref[...]
    mod_in = lambda a: _modulate(a, g1_ref[...], mods[0], mods[1])
    zeros = jnp.zeros((hb, D), F32)
    lhs = jnp.concatenate([jnp.where(has_prev, mod_in(prev_ref[...]), zeros), mod_in(x),
                           jnp.where(has_next, mod_in(next_ref[...]), zeros)],
                          axis=0).astype(BF16)
    acc = jnp.zeros((TM, D), F32)
    for c in range(D // HC):
        cs = slice(c * HC, (c + 1) * HC)
        gb = _dot(lhs, win_ref[:, c * HC:(c + 1) * HC])
        gc = _dot(lhs, win_ref[:, D + c * HC:D + (c + 1) * HC])
        u = _dot(lhs, win_ref[:, 2 * D + c * HC:2 * D + (c + 1) * HC])
        p_ref[...] = gc * u
        mixed = gb[hb:hb + TM] * _conv3(p_ref, cw_ref[:, cs], hb)
        acc = acc + _dot(mixed.astype(BF16), wout_ref[cs, :])
    y, hf = _residual_and_ffn_in(x, acc, mods, g2_ref[...])
    y_ref[...] = y
    hf_ref[...] = hf


def _sconv(x, mods, layer, g1, win, cw, wout, g2):
    hb = HALO_F32
    r = TM // hb
    return pl.pallas_call(
        _sconv_kernel,
        out_shape=(jax.ShapeDtypeStruct((T, D), F32), jax.ShapeDtypeStruct((T, D), BF16)),
        grid=(NB,),
        in_specs=[_row_spec(D),
                  pl.BlockSpec((hb, D), lambda i: (jnp.maximum(i * r - 1, 0), 0)),
                  pl.BlockSpec((hb, D), lambda i: (jnp.minimum((i + 1) * r, T // hb - 1), 0)),
                  _mod_spec(layer), _full((1, D)), _full((D, 3 * D)), _full((3, D)),
                  _full((D, D)), _full((1, D))],
        out_specs=(_row_spec(D), _row_spec(D)),
        scratch_shapes=[pltpu.VMEM((TM + 2 * hb, HC), F32)],
        compiler_params=_params(),
        name="sconv",
    )(x, x, x, mods, g1, win, cw, wout, g2)


def _gmlp_kernel(x_ref, mod_ref, g1_ref, win_ref, vn_ref, ws_ref, bs_ref, wout_ref, g2_ref,
                 y_ref, hf_ref, gated_ref):
    mods = _mod_parts(mod_ref)
    x = x_ref[...]
    h = _modulate(x, g1_ref[...], mods[0], mods[1]).astype(BF16)
    z = _dot(h, win_ref[...])
    z = z * (0.5 * (1.0 + jnp.tanh(math.sqrt(2.0 / math.pi) * (z + 0.044715 * (z * z * z)))))
    u = z[:, :D]
    v = _rms(z[:, D:], vn_ref[...]).astype(BF16)
    bs = bs_ref[...]
    for g in range(GMLP_GROUPS):
        cs = slice(g * LANE, (g + 1) * LANE)
        w = ws_ref[g]
        for r in range(TM // GMLP_CHUNK):
            rs = slice(r * GMLP_CHUNK, (r + 1) * GMLP_CHUNK)
            mixed = _dot(w, v[rs, cs]) + bs[:, g:g + 1]
            gated_ref[rs, cs] = (u[rs, cs] * mixed).astype(BF16)
    y, hf = _residual_and_ffn_in(x, _dot(gated_ref[...], wout_ref[...]), mods, g2_ref[...])
    y_ref[...] = y
    hf_ref[...] = hf


def _gmlp(x, mods, layer, g1, win, vn, ws, bs_t, wout, g2):
    return pl.pallas_call(
        _gmlp_kernel,
        out_shape=(jax.ShapeDtypeStruct((T, D), F32), jax.ShapeDtypeStruct((T, D), BF16)),
        grid=(NB,),
        in_specs=[_row_spec(D), _mod_spec(layer), _full((1, D)), _full((D, 2 * D)), _full((1, D)),
                  _full((GMLP_GROUPS, GMLP_CHUNK, GMLP_CHUNK)), _full((GMLP_CHUNK, GMLP_GROUPS)),
                  _full((D, D)), _full((1, D))],
        out_specs=(_row_spec(D), _row_spec(D)),
        scratch_shapes=[pltpu.VMEM((TM, D), BF16)],
        compiler_params=_params(),
        name="gmlp",
    )(x, mods, g1, win, vn, ws, bs_t, wout, g2)


def _rope_tables():
    rows = DEC_SEQ // GRID_W
    row = jnp.repeat(jnp.arange(rows, dtype=F32), GRID_W)
    col = jnp.tile(jnp.arange(GRID_W, dtype=F32), rows)
    n_freq = MLA_ROPE // 4
    inv_freq = ROPE_THETA ** (-jnp.arange(n_freq, dtype=F32) / n_freq)
    ang = jnp.concatenate([row[:, None] * inv_freq, col[:, None] * inv_freq], axis=-1)
    cos, sin = jnp.cos(ang), jnp.sin(ang)
    cos = jnp.concatenate([cos, jnp.ones((TM, cos.shape[1]), F32)], axis=0)
    sin = jnp.concatenate([sin, jnp.zeros((TM, sin.shape[1]), F32)], axis=0)
    return jnp.tile(cos, (1, 4)), jnp.tile(jnp.concatenate([-sin, sin], axis=1), (1, 2))


def _pad_lanes(a, width=LANE):
    return jnp.pad(a, [(0, 0)] * (a.ndim - 1) + [(0, width - a.shape[-1])])


def _kv_rows(cache, new):
    w = new.shape[-1]
    lat = jnp.concatenate([cache, new[T_P:].reshape(DEC_BATCH, DEC_SEQ, w)], axis=1)
    return jnp.concatenate([lat.reshape(DEC_BATCH * KV_LEN, w), new[:T_P]], axis=0)


def _heads_first(a):
    return jnp.swapaxes(a, -3, -2).reshape(a.shape[:-3] + (D,))


def kernel(x_prompt, x_sample, cache_mla_ckv, cache_mla_krope, cache_diff_k, cache_diff_v, c, c_ctx, ada_w, ada_b, norm1_g, norm2_g, mla_w_down, mla_q_norm, mla_kv_norm, mla_w_uq, mla_w_uk, mla_w_uv, mla_qn_nope, mla_qn_rope, mla_kn_nope, mla_kn_rope, mla_w_o, diff_w_qkv, diff_qn, diff_kn, diff_lq1, diff_lk1, diff_lq2, diff_lk2, diff_head_norm, diff_w_o, sconv_w_in, sconv_w, sconv_w_out, gmlp_w_in, gmlp_v_norm, gmlp_w_s, gmlp_b_s, gmlp_w_out, ffn_w_in, ffn_conv_w, ffn_conv_b, ffn_w_out):
    x = jnp.concatenate([x_prompt.reshape(T_P, D), x_sample.reshape(T_S, D)], axis=0)
    cond = jnp.concatenate([c_ctx[None], c, jnp.zeros((MOD_ROWS - 1 - DEC_BATCH, D), F32)], axis=0)
    mods = _adaln(cond, ada_w, ada_b).reshape(DEPTH * MOD_ROWS, 1, N_MOD * D)
    cos_t, sin_t = _rope_tables()
    row = lambda a: a.reshape(1, -1)

    def ffn(y, hf, i):
        return _ffn(y, hf, mods, i, ffn_w_in[i].astype(BF16), ffn_conv_w[i], row(ffn_conv_b[i]),
                    ffn_w_out[i].astype(BF16))

    wd = _pad_lanes(mla_w_down[0], MLA_DOWN_W).astype(BF16)
    wuq = _pad_lanes(mla_w_uq[0].reshape(MLA_Q_LORA, MLA_HEADS, MLA_NOPE + MLA_ROPE), MLA_DK)
    wuq = wuq.reshape(MLA_Q_LORA, MLA_HEADS * MLA_DK).astype(BF16)
    q, ckv, kr = _mla_in(x, mods, 0, row(norm1_g[0]), wd, row(mla_q_norm[0]), row(mla_kv_norm[0]),
                         _pad_lanes(row(mla_kn_rope[0])), wuq, row(mla_qn_nope[0]),
                         _pad_lanes(row(mla_qn_rope[0])), cos_t, sin_t)
    state_ckv = ckv[:T_P].reshape(BATCH, 1, SEQ, MLA_KV_LORA)
    state_kr = kr[:T_P, :MLA_ROPE].reshape(BATCH, 1, SEQ, MLA_ROPE)
    k, v = _mla_kv(_kv_rows(cache_mla_ckv[:, 0], ckv), _kv_rows(_pad_lanes(cache_mla_krope[:, 0]), kr),
                   mla_w_uk[0].astype(BF16), mla_w_uv[0].astype(BF16), row(mla_kn_nope[0]))
    o = _attention(_mla_attn_kernel, "mla_attn", q, k, v, [], MLA_HEADS * MLA_DK, MLA_HEADS * MLA_V)
    y, hf = _mix_out(x, o, mods, 0, mla_w_o[0].astype(BF16), row(norm2_g[0]))
    x = ffn(y, hf, 0)

    lam_init = 0.8 - 0.6 * math.exp(-0.3 * 1)
    wq, wk, wv = jnp.split(diff_w_qkv[0], 3, axis=-1)
    perm = lambda w: _heads_first(w.reshape(D, 2, DIFF_HEADS, DIFF_HD))
    w_qkv = jnp.concatenate([perm(wq), perm(wk), wv], axis=-1).astype(BF16)
    pair = lambda g: row(jnp.tile(g, 2))
    q, k, v, kst, vst = _diff_in(x, mods, 1, row(norm1_g[1]), w_qkv, pair(diff_qn[0]), pair(diff_kn[0]),
                                 cos_t, sin_t)
    state_dk = jnp.swapaxes(kst.reshape(BATCH, SEQ, DIFF_HEADS, 2, DIFF_HD), 2, 3)[:, None]
    state_dv = vst.reshape(BATCH, 1, SEQ, DIFF_HEADS, 2 * DIFF_HD)
    k_all = _kv_rows(_heads_first(cache_diff_k[:, 0]).astype(BF16), k)
    v_all = _kv_rows(cache_diff_v[:, 0].reshape(DEC_BATCH, PAST, D).astype(BF16), v)
    smalls = [row(diff_lq1[0]), row(diff_lk1[0]), row(diff_lq2[0]), row(diff_lk2[0]),
              row(diff_head_norm[0])]
    o = _attention(functools.partial(_diff_attn_kernel, lam_init), "diff_attn", q, k_all, v_all,
                   smalls, D, D)
    y, hf = _mix_out(x, o, mods, 1, diff_w_o[0].astype(BF16), row(norm2_g[1]))
    x = ffn(y, hf, 1)

    y, hf = _sconv(x, mods, 2, row(norm1_g[2]), sconv_w_in[0].astype(BF16), sconv_w[0],
                   sconv_w_out[0].astype(BF16), row(norm2_g[2]))
    x = ffn(y, hf, 2)

    y, hf = _gmlp(x, mods, 3, row(norm1_g[3]), gmlp_w_in[0].astype(BF16), row(gmlp_v_norm[0]),
                  gmlp_w_s[0].astype(BF16), gmlp_b_s[0].T, gmlp_w_out[0].astype(BF16), row(norm2_g[3]))
    x = ffn(y, hf, 3)

    return (x[:T_P].reshape(BATCH, SEQ, D), x[T_P:].reshape(DEC_BATCH, DEC_SEQ, D),
            state_ckv, state_kr, state_dk, state_dv)
```

```python
import functools
import math

import jax
import jax.numpy as jnp
from jax import lax
from jax.experimental import pallas as pl
from jax.experimental.pallas import tpu as pltpu

D = 1024
BATCH, SEQ = 16, 256
DEC_BATCH, DEC_SEQ = 8, 2048
PAST = 256
DEPTH = 4
GRID_W = 64
EPS = 1e-6
ROPE_THETA = 10000.0
N_MOD = 6
MLA_HEADS, MLA_NOPE, MLA_ROPE, MLA_V = 8, 128, 64, 128
MLA_Q_LORA, MLA_KV_LORA = 768, 256
DIFF_HEADS, DIFF_HD = 8, 64
GMLP_CHUNK, GMLP_GROUPS = 128, 8
FFN_HIDDEN = 2816

T_P = BATCH * SEQ
T_S = DEC_BATCH * DEC_SEQ
T = T_P + T_S
TM = 256
NB_P = T_P // TM
NB = T // TM
BPS = DEC_SEQ // TM
KV_LEN = PAST + DEC_SEQ
T_KV = DEC_BATCH * KV_LEN + T_P
MOD_ROWS = 16
LANE = 128
MLA_DK = 256
HC = 256
HALO_BF16 = 16
HALO_F32 = 8
VMEM_LIMIT = 56 * 1024 * 1024

F32 = jnp.float32
BF16 = jnp.bfloat16


def _params(n_axes=1, vmem=VMEM_LIMIT):
    return pltpu.CompilerParams(dimension_semantics=("arbitrary",) * n_axes,
                                vmem_limit_bytes=vmem)


def _dot(a, b):
    return jnp.dot(a, b, preferred_element_type=F32)


def _rms(x, g):
    ms = jnp.mean(x * x, axis=-1, keepdims=True)
    return (x * lax.rsqrt(ms + EPS)) * g


def _modulate(x, g, shift, scale):
    return _rms(x, g) * (1.0 + scale) + shift


def _sigmoid(x):
    return 1.0 / (1.0 + jnp.exp(-x))


def _mod_parts(mod_ref):
    m = mod_ref[0]
    return [m[:, k * D:(k + 1) * D] for k in range(N_MOD)]


def _full(shape):
    return pl.BlockSpec(shape, lambda *_: (0,) * len(shape))


def _row_spec(width, rows=TM):
    return pl.BlockSpec((rows, width), lambda i: (i, 0))


def _split_specs(width):
    return [pl.BlockSpec((TM, width), lambda i: (jnp.minimum(i, NB_P - 1), 0)),
            pl.BlockSpec((TM, width), lambda i: (jnp.maximum(i - NB_P, 0), 0))]


def _split_rows(p_ref, s_ref):
    return jnp.where(pl.program_id(0) < NB_P, p_ref[...], s_ref[...])


def _mod_spec(layer):
    def imap(i):
        j = jnp.maximum(i - NB_P, 0)
        return (layer * MOD_ROWS + jnp.where(i < NB_P, 0, 1 + j // BPS), 0, 0)
    return pl.BlockSpec((1, 1, N_MOD * D), imap)


def _rope_spec():
    def imap(i):
        j = jnp.maximum(i - NB_P, 0)
        return (jnp.where(i < NB_P, BPS, j % BPS), 0)
    return pl.BlockSpec((TM, LANE), imap)


def _seq_edges(i):
    row0 = i * TM
    seq_len = jnp.where(row0 < T_P, SEQ, DEC_SEQ)
    pos = row0 % seq_len
    return pos != 0, pos + TM != seq_len


def _group_rms(xb, gain, left):
    sq = xb * xb
    sl = jnp.sum(jnp.where(left, sq, 0.0), axis=-1, keepdims=True)
    sr = jnp.sum(jnp.where(left, 0.0, sq), axis=-1, keepdims=True)
    ms = jnp.where(left, sl, sr) * (1.0 / 64.0)
    return (xb * lax.rsqrt(ms + EPS)) * gain


def _rope128(xb, cos, sin, first):
    swapped = jnp.where(first, pltpu.roll(xb, LANE - 32, 1), pltpu.roll(xb, 32, 1))
    return xb * cos + swapped * sin


ADA_TN = 1536


def _adaln_kernel(cond_ref, w_ref, b_ref, o_ref):
    c = cond_ref[...]
    a = (c * _sigmoid(c)).astype(BF16)
    o_ref[0] = _dot(a, w_ref[0].astype(BF16)) + b_ref[0]


def _adaln(cond, ada_w, ada_b):
    n = N_MOD * D
    return pl.pallas_call(
        _adaln_kernel,
        out_shape=jax.ShapeDtypeStruct((DEPTH, MOD_ROWS, n), F32),
        grid=(DEPTH, n // ADA_TN),
        in_specs=[pl.BlockSpec((MOD_ROWS, D), lambda l, j: (0, 0)),
                  pl.BlockSpec((1, D, ADA_TN), lambda l, j: (l, 0, j)),
                  pl.BlockSpec((1, 1, ADA_TN), lambda l, j: (l, 0, j))],
        out_specs=pl.BlockSpec((1, MOD_ROWS, ADA_TN), lambda l, j: (l, 0, j)),
        compiler_params=_params(2),
        name="adaln",
    )(cond, ada_w, ada_b.reshape(DEPTH, 1, n))


MLA_DOWN_W = MLA_Q_LORA + MLA_KV_LORA + LANE


def _mla_in_kernel(xp_ref, xs_ref, mod_ref, g1_ref, wd_ref, qn_ref, kvn_ref, knr_ref, wuq_ref,
                   qnn_ref, qnr_ref, cos_ref, sin_ref, q_ref, ckv_ref, kr_ref):
    shift, scale = _mod_parts(mod_ref)[:2]
    h = _modulate(_split_rows(xp_ref, xs_ref), g1_ref[...], shift, scale).astype(BF16)
    d = _dot(h, wd_ref[...])
    cq = _rms(d[:, :MLA_Q_LORA], qn_ref[...]).astype(BF16)
    ckv_ref[...] = _rms(d[:, MLA_Q_LORA:MLA_Q_LORA + MLA_KV_LORA], kvn_ref[...])

    lane = lax.broadcasted_iota(jnp.int32, (1, LANE), 1)
    first = (lane % 64) < 32
    cos, sin = cos_ref[...], sin_ref[...]

    def rope_part(xb, gain):
        ms = jnp.sum(xb * xb, axis=-1, keepdims=True) * (1.0 / MLA_ROPE)
        return _rope128((xb * lax.rsqrt(ms + EPS)) * gain, cos, sin, first)

    kr_ref[...] = rope_part(d[:, MLA_Q_LORA + MLA_KV_LORA:], knr_ref[...])

    q = _dot(cq, wuq_ref[...])
    for hh in range(MLA_HEADS):
        c0 = hh * MLA_DK
        q_ref[:, c0:c0 + LANE] = _rms(q[:, c0:c0 + LANE], qnn_ref[...]).astype(BF16)
        q_ref[:, c0 + LANE:c0 + MLA_DK] = rope_part(q[:, c0 + LANE:c0 + MLA_DK],
                                                    qnr_ref[...]).astype(BF16)


def _mla_in(xp, xs, mods, layer, g1, wd, qn, kvn, knr, wuq, qnn, qnr, cos_t, sin_t):
    return pl.pallas_call(
        _mla_in_kernel,
        out_shape=(jax.ShapeDtypeStruct((T, MLA_HEADS * MLA_DK), BF16),
                   jax.ShapeDtypeStruct((T, MLA_KV_LORA), F32),
                   jax.ShapeDtypeStruct((T, LANE), F32)),
        grid=(NB,),
        in_specs=[*_split_specs(D), _mod_spec(layer), _full((1, D)), _full((D, MLA_DOWN_W)),
                  _full((1, MLA_Q_LORA)), _full((1, MLA_KV_LORA)), _full((1, LANE)),
                  _full((MLA_Q_LORA, MLA_HEADS * MLA_DK)), _full((1, LANE)), _full((1, LANE)),
                  _rope_spec(), _rope_spec()],
        out_specs=(_row_spec(MLA_HEADS * MLA_DK), _row_spec(MLA_KV_LORA), _row_spec(LANE)),
        compiler_params=_params(),
        name="mla_in",
    )(xp, xs, mods, g1, wd, qn, kvn, knr, wuq, qnn, qnr, cos_t, sin_t)


def _mla_kv_kernel(ckv_ref, kr_ref, wuk_ref, wuv_ref, knn_ref, k_ref, v_ref):
    c = ckv_ref[...].astype(BF16)
    kn = _dot(c, wuk_ref[...])
    v_ref[...] = _dot(c, wuv_ref[...]).astype(BF16)
    kr = kr_ref[...].astype(BF16)
    for hh in range(MLA_HEADS):
        c0 = hh * MLA_DK
        k_ref[:, c0:c0 + LANE] = _rms(kn[:, hh * LANE:(hh + 1) * LANE], knn_ref[...]).astype(BF16)
        k_ref[:, c0 + LANE:c0 + MLA_DK] = kr


def _mla_kv(ckv_all, kr_all, wuk, wuv, knn):
    return pl.pallas_call(
        _mla_kv_kernel,
        out_shape=(jax.ShapeDtypeStruct((T_KV, MLA_HEADS * MLA_DK), BF16),
                   jax.ShapeDtypeStruct((T_KV, MLA_HEADS * MLA_V), BF16)),
        grid=(T_KV // TM,),
        in_specs=[_row_spec(MLA_KV_LORA), _row_spec(LANE),
                  _full((MLA_KV_LORA, MLA_HEADS * MLA_NOPE)), _full((MLA_KV_LORA, MLA_HEADS * MLA_V)),
                  _full((1, LANE))],
        out_specs=(_row_spec(MLA_HEADS * MLA_DK), _row_spec(MLA_HEADS * MLA_V)),
        compiler_params=_params(),
        name="mla_kv",
    )(ckv_all, kr_all, wuk, wuv, knn)


LOG2E = math.log2(math.e)


def _softmax_pv(s2, v_ones):
    p = jnp.exp2(s2 - jnp.max(s2, axis=-1, keepdims=True))
    o = _dot(p.astype(BF16), v_ones)
    return o[:, :LANE] * (1.0 / o[:, LANE:LANE + 1])


def _with_ones(v):
    return jnp.concatenate([v, jnp.ones_like(v)], axis=1)


def _mla_attn_kernel(q_ref, kt_ref, v_ref, *rest):
    o_ref = rest[-1]
    scale2 = (MLA_NOPE + MLA_ROPE) ** -0.5 * LOG2E
    for hh in range(MLA_HEADS):
        ks = slice(hh * MLA_DK, (hh + 1) * MLA_DK)
        vs = slice(hh * MLA_V, (hh + 1) * MLA_V)
        o = _softmax_pv(_dot(q_ref[:, ks], kt_ref[0, ks, :]) * scale2, _with_ones(v_ref[:, vs]))
        o_ref[:, vs] = o.astype(BF16)


def _diff_attn_kernel(lam_init, q_ref, kt_ref, v_ref, lq1_ref, lk1_ref, lq2_ref, lk2_ref,
                      hn_ref, *rest):
    o_ref = rest[-1]
    lam = (jnp.exp(jnp.sum(lq1_ref[...] * lk1_ref[...], axis=-1, keepdims=True))
           - jnp.exp(jnp.sum(lq2_ref[...] * lk2_ref[...], axis=-1, keepdims=True)) + lam_init)
    lane = lax.broadcasted_iota(jnp.int32, (1, LANE), 1)
    left = lane < DIFF_HD
    zero = jnp.zeros((), BF16)
    for hh in range(DIFF_HEADS):
        cs = slice(hh * LANE, (hh + 1) * LANE)
        qb, kt, v1 = q_ref[:, cs], kt_ref[0, cs, :], _with_ones(v_ref[:, cs])
        o0 = _softmax_pv(_dot(jnp.where(left, qb, zero), kt) * LOG2E, v1)
        o1 = _softmax_pv(_dot(jnp.where(left, zero, qb), kt) * LOG2E, v1)
        o = o0 - lam * o1
        o_ref[:, cs] = (_rms(o, hn_ref[...]) * (1.0 - lam_init)).astype(BF16)


def _attention(body, name, q, k, v, smalls, dk, dv):
    n_lat = DEC_BATCH * KV_LEN
    kt_lat = jnp.swapaxes(k[:n_lat].reshape(DEC_BATCH, KV_LEN, dk), 1, 2)
    kt_prompt = jnp.swapaxes(k[n_lat:].reshape(BATCH, SEQ, dk), 1, 2)

    def call(batches, sk, q_off, v_off, kt, prev):
        nq = (SEQ if prev is None else DEC_SEQ) // TM
        in_specs = [pl.BlockSpec((TM, dk), lambda b, i: (q_off + b * nq + i, 0)),
                    pl.BlockSpec((1, dk, sk), lambda b, i: (b, 0, 0)),
                    pl.BlockSpec((sk, dv), lambda b, i: (v_off + b, 0))]
        in_specs += [_full(s.shape) for s in smalls]
        args = [q, kt, v, *smalls]
        aliases = {}
        if prev is not None:
            in_specs.append(pl.BlockSpec(memory_space=pl.ANY))
            args.append(prev)
            aliases = {len(args) - 1: 0}
        return pl.pallas_call(
            body,
            out_shape=jax.ShapeDtypeStruct((T, dv), BF16),
            grid=(batches, nq),
            in_specs=in_specs,
            out_specs=pl.BlockSpec((TM, dv), lambda b, i: (q_off + b * nq + i, 0)),
            input_output_aliases=aliases,
            compiler_params=_params(2),
            name=name + ("_prompt" if prev is None else "_latent"),
        )(*args)

    o = call(BATCH, SEQ, 0, n_lat // SEQ, kt_prompt, None)
    return call(DEC_BATCH, KV_LEN, NB_P, 0, kt_lat, o)


def _residual_and_ffn_in(x, mix, mods, g2):
    _, _, gate1, shift2, scale2, _ = mods
    y = x + gate1 * mix
    return y, _modulate(y, g2, shift2, scale2).astype(BF16)


def _mix_out_kernel(n_x, *refs):
    x_refs, (o_ref, mod_ref, wo_ref, g2_ref, y_ref, hf_ref) = refs[:n_x], refs[n_x:]
    x = x_refs[0][...] if n_x == 1 else _split_rows(*x_refs)
    mix = _dot(o_ref[...], wo_ref[...])
    y, hf = _residual_and_ffn_in(x, mix, _mod_parts(mod_ref), g2_ref[...])
    y_ref[...] = y
    hf_ref[...] = hf


def _mix_out(xs, o, mods, layer, wo, g2):
    x_specs = [_row_spec(D)] if len(xs) == 1 else _split_specs(D)
    return pl.pallas_call(
        functools.partial(_mix_out_kernel, len(xs)),
        out_shape=(jax.ShapeDtypeStruct((T, D), F32), jax.ShapeDtypeStruct((T, D), BF16)),
        grid=(NB,),
        in_specs=[*x_specs, _row_spec(D), _mod_spec(layer), _full((D, D)), _full((1, D))],
        out_specs=(_row_spec(D), _row_spec(D)),
        compiler_params=_params(),
        name="mix_out",
    )(*xs, o, mods, wo, g2)


def _conv3(zs_ref, w, off):
    z = zs_ref[...]
    n = z.shape[0]
    z_prev = pltpu.roll(z, 1, 0)[off:off + TM]
    z_next = pltpu.roll(z, n - 1, 0)[off:off + TM]
    return z_prev * w[0:1] + z[off:off + TM] * w[1:2] + z_next * w[2:3]


def _ffn_kernel(n_out, y_ref, hf_ref, prev_ref, next_ref, mod_ref, win_ref, cw_ref, cb_ref, wout_ref,
                *rest):
    out_refs, (lhs_ref, z_ref, act_ref) = rest[:n_out], rest[n_out:]
    has_prev, has_next = _seq_edges(pl.program_id(0))
    hb = HALO_BF16
    zeros = jnp.zeros((hb, D), BF16)
    lhs_ref[0:hb, :] = jnp.where(has_prev, prev_ref[...], zeros)
    lhs_ref[hb:hb + TM, :] = hf_ref[...]
    lhs_ref[hb + TM:, :] = jnp.where(has_next, next_ref[...], zeros)
    lhs = lhs_ref[...]
    for c in range(FFN_HIDDEN // HC):
        gs = slice(c * HC, (c + 1) * HC)
        us = slice(FFN_HIDDEN + c * HC, FFN_HIDDEN + (c + 1) * HC)
        z_ref[:, gs] = _dot(lhs, win_ref[:, gs])
        z_ref[:, us] = _dot(lhs, win_ref[:, us])
        g = _conv3(z_ref.at[:, gs], cw_ref[:, gs], hb) + cb_ref[:, gs]
        u = _conv3(z_ref.at[:, us], cw_ref[:, us], hb) + cb_ref[:, us]
        act_ref[:, gs] = ((g * _sigmoid(g)) * u).astype(BF16)
    gate2 = _mod_parts(mod_ref)[5]
    out = y_ref[...] + gate2 * _dot(act_ref[...], wout_ref[...])
    if len(out_refs) == 1:
        out_refs[0][...] = out
    else:
        @pl.when(pl.program_id(0) < NB_P)
        def _():
            out_refs[0][...] = out

        @pl.when(pl.program_id(0) >= NB_P)
        def _():
            out_refs[1][...] = out


def _ffn(y, hf, mods, layer, win, cw, cb, wout, split_out=False):
    hb = HALO_BF16
    r = TM // hb
    if split_out:
        out_shape = (jax.ShapeDtypeStruct((T_P, D), F32), jax.ShapeDtypeStruct((T_S, D), F32))
        out_specs = tuple(_split_specs(D))
    else:
        out_shape, out_specs = (jax.ShapeDtypeStruct((T, D), F32),), (_row_spec(D),)
    return pl.pallas_call(
        functools.partial(_ffn_kernel, len(out_shape)),
        out_shape=out_shape,
        grid=(NB,),
        in_specs=[_row_spec(D), _row_spec(D),
                  pl.BlockSpec((hb, D), lambda i: (jnp.maximum(i * r - 1, 0), 0)),
                  pl.BlockSpec((hb, D), lambda i: (jnp.minimum((i + 1) * r, T // hb - 1), 0)),
                  _mod_spec(layer), _full((D, 2 * FFN_HIDDEN)), _full((3, 2 * FFN_HIDDEN)),
                  _full((1, 2 * FFN_HIDDEN)), _full((FFN_HIDDEN, D))],
        out_specs=out_specs,
        scratch_shapes=[pltpu.VMEM((TM + 2 * hb, D), BF16),
                        pltpu.VMEM((TM + 2 * hb, 2 * FFN_HIDDEN), F32),
                        pltpu.VMEM((TM, FFN_HIDDEN), BF16)],
        compiler_params=_params(),
        name="ffn",
    )(y, hf, hf, hf, mods, win, cw, cb, wout)


def _diff_in_kernel(x_ref, mod_ref, g1_ref, w_ref, qn_ref, kn_ref, cos_ref, sin_ref,
                    q_ref, k_ref, v_ref, kst_ref, vst_ref, kn_ref_scratch):
    shift, scale = _mod_parts(mod_ref)[:2]
    h = _modulate(x_ref[...], g1_ref[...], shift, scale).astype(BF16)
    z = _dot(h, w_ref[...])
    lane = lax.broadcasted_iota(jnp.int32, (1, LANE), 1)
    left = lane < DIFF_HD
    first = (lane % 64) < 32
    cos, sin = cos_ref[...], sin_ref[...]
    v = z[:, 2 * D:]
    v_ref[...] = v.astype(BF16)
    for hh in range(DIFF_HEADS):
        cs = slice(hh * LANE, (hh + 1) * LANE)
        qn = _group_rms(z[:, hh * LANE:(hh + 1) * LANE], qn_ref[...], left)
        q_ref[:, cs] = (_rope128(qn, cos, sin, first) * DIFF_HD ** -0.5).astype(BF16)
        kn = _group_rms(z[:, D + hh * LANE:D + (hh + 1) * LANE], kn_ref[...], left)
        k_ref[:, cs] = _rope128(kn, cos, sin, first).astype(BF16)
        kn_ref_scratch[:, cs] = kn

    @pl.when(pl.program_id(0) < NB_P)
    def _():
        vst_ref[...] = v
        kst_ref[...] = kn_ref_scratch[...]


def _diff_in(x, mods, layer, g1, w, qn, kn, cos_t, sin_t):
    st_spec = pl.BlockSpec((TM, D), lambda i: (jnp.minimum(i, NB_P - 1), 0))
    return pl.pallas_call(
        _diff_in_kernel,
        out_shape=(jax.ShapeDtypeStruct((T, D), BF16), jax.ShapeDtypeStruct((T, D), BF16),
                   jax.ShapeDtypeStruct((T, D), BF16), jax.ShapeDtypeStruct((T_P, D), F32),
                   jax.ShapeDtypeStruct((T_P, D), F32)),
        grid=(NB,),
        in_specs=[_row_spec(D), _mod_spec(layer), _full((1, D)), _full((D, 3 * D)),
                  _full((1, LANE)), _full((1, LANE)), _rope_spec(), _rope_spec()],
        out_specs=(_row_spec(D), _row_spec(D), _row_spec(D), st_spec, st_spec),
        scratch_shapes=[pltpu.VMEM((TM, D), F32)],
        compiler_params=_params(),
        name="diff_in",
    )(x, mods, g1, w, qn, kn, cos_t, sin_t)


def _sconv_kernel(x_ref, prev_ref, next_ref, mod_ref, g1_ref, win_ref, cw_ref, wout_ref, g2_ref,
                  y_ref, hf_ref, p_ref):
    has_prev, has_next = _seq_edges(pl.program_id(0))
    hb = HALO_F32
    mods = _mod_parts(mod_ref)
    x = x_ref[...]
    mod_in = lambda a: _modulate(a, g1_ref[...], mods[0], mods[1])
    zeros = jnp.zeros((hb, D), F32)
    lhs = jnp.concatenate([jnp.where(has_prev, mod_in(prev_ref[...]), zeros), mod_in(x),
                           jnp.where(has_next, mod_in(next_ref[...]), zeros)],
                          axis=0).astype(BF16)
    acc = jnp.zeros((TM, D), F32)
    for c in range(D // HC):
        cs = slice(c * HC, (c + 1) * HC)
        gb = _dot(lhs, win_ref[:, c * HC:(c + 1) * HC])
        gc = _dot(lhs, win_ref[:, D + c * HC:D + (c + 1) * HC])
        u = _dot(lhs, win_ref[:, 2 * D + c * HC:2 * D + (c + 1) * HC])
        p_ref[...] = gc * u
        mixed = gb[hb:hb + TM] * _conv3(p_ref, cw_ref[:, cs], hb)
        acc = acc + _dot(mixed.astype(BF16), wout_ref[cs, :])
    y, hf = _residual_and_ffn_in(x, acc, mods, g2_ref[...])
    y_ref[...] = y
    hf_ref[...] = hf


def _sconv(x, mods, layer, g1, win, cw, wout, g2):
    hb = HALO_F32
    r = TM // hb
    return pl.pallas_call(
        _sconv_kernel,
        out_shape=(jax.ShapeDtypeStruct((T, D), F32), jax.ShapeDtypeStruct((T, D), BF16)),
        grid=(NB,),
        in_specs=[_row_spec(D),
                  pl.BlockSpec((hb, D), lambda i: (jnp.maximum(i * r - 1, 0), 0)),
                  pl.BlockSpec((hb, D), lambda i: (jnp.minimum((i + 1) * r, T // hb - 1), 0)),
                  _mod_spec(layer), _full((1, D)), _full((D, 3 * D)), _full((3, D)),
                  _full((D, D)), _full((1, D))],
        out_specs=(_row_spec(D), _row_spec(D)),
        scratch_shapes=[pltpu.VMEM((TM + 2 * hb, HC), F32)],
        compiler_params=_params(),
        name="sconv",
    )(x, x, x, mods, g1, win, cw, wout, g2)


def _gmlp_kernel(x_ref, mod_ref, g1_ref, win_ref, vn_ref, ws_ref, bs_ref, wout_ref, g2_ref,
                 y_ref, hf_ref, gated_ref):
    mods = _mod_parts(mod_ref)
    x = x_ref[...]
    h = _modulate(x, g1_ref[...], mods[0], mods[1]).astype(BF16)
    z = _dot(h, win_ref[...])
    z = z * (0.5 * (1.0 + jnp.tanh(math.sqrt(2.0 / math.pi) * (z + 0.044715 * (z * z * z)))))
    u = z[:, :D]
    v = _rms(z[:, D:], vn_ref[...]).astype(BF16)
    bs = bs_ref[...]
    for g in range(GMLP_GROUPS):
        cs = slice(g * LANE, (g + 1) * LANE)
        w = ws_ref[g]
        for r in range(TM // GMLP_CHUNK):
            rs = slice(r * GMLP_CHUNK, (r + 1) * GMLP_CHUNK)
            mixed = _dot(w, v[rs, cs]) + bs[:, g:g + 1]
            gated_ref[rs, cs] = (u[rs, cs] * mixed).astype(BF16)
    y, hf = _residual_and_ffn_in(x, _dot(gated_ref[...], wout_ref[...]), mods, g2_ref[...])
    y_ref[...] = y
    hf_ref[...] = hf


def _gmlp(x, mods, layer, g1, win, vn, ws, bs_t, wout, g2):
    return pl.pallas_call(
        _gmlp_kernel,
        out_shape=(jax.ShapeDtypeStruct((T, D), F32), jax.ShapeDtypeStruct((T, D), BF16)),
        grid=(NB,),
        in_specs=[_row_spec(D), _mod_spec(layer), _full((1, D)), _full((D, 2 * D)), _full((1, D)),
                  _full((GMLP_GROUPS, GMLP_CHUNK, GMLP_CHUNK)), _full((GMLP_CHUNK, GMLP_GROUPS)),
                  _full((D, D)), _full((1, D))],
        out_specs=(_row_spec(D), _row_spec(D)),
        scratch_shapes=[pltpu.VMEM((TM, D), BF16)],
        compiler_params=_params(),
        name="gmlp",
    )(x, mods, g1, win, vn, ws, bs_t, wout, g2)


def _rope_tables():
    rows = DEC_SEQ // GRID_W
    row = jnp.repeat(jnp.arange(rows, dtype=F32), GRID_W)
    col = jnp.tile(jnp.arange(GRID_W, dtype=F32), rows)
    n_freq = MLA_ROPE // 4
    inv_freq = ROPE_THETA ** (-jnp.arange(n_freq, dtype=F32) / n_freq)
    ang = jnp.concatenate([row[:, None] * inv_freq, col[:, None] * inv_freq], axis=-1)
    cos, sin = jnp.cos(ang), jnp.sin(ang)
    cos = jnp.concatenate([cos, jnp.ones((TM, cos.shape[1]), F32)], axis=0)
    sin = jnp.concatenate([sin, jnp.zeros((TM, sin.shape[1]), F32)], axis=0)
    return jnp.tile(cos, (1, 4)), jnp.tile(jnp.concatenate([-sin, sin], axis=1), (1, 2))


def _pad_lanes(a, width=LANE):
    return jnp.pad(a, [(0, 0)] * (a.ndim - 1) + [(0, width - a.shape[-1])])


def _kv_rows(cache, new):
    w = new.shape[-1]
    lat = jnp.concatenate([cache, new[T_P:].reshape(DEC_BATCH, DEC_SEQ, w)], axis=1)
    return jnp.concatenate([lat.reshape(DEC_BATCH * KV_LEN, w), new[:T_P]], axis=0)


def _heads_first(a):
    return jnp.swapaxes(a, -3, -2).reshape(a.shape[:-3] + (D,))


def kernel(x_prompt, x_sample, cache_mla_ckv, cache_mla_krope, cache_diff_k, cache_diff_v, c, c_ctx, ada_w, ada_b, norm1_g, norm2_g, mla_w_down, mla_q_norm, mla_kv_norm, mla_w_uq, mla_w_uk, mla_w_uv, mla_qn_nope, mla_qn_rope, mla_kn_nope, mla_kn_rope, mla_w_o, diff_w_qkv, diff_qn, diff_kn, diff_lq1, diff_lk1, diff_lq2, diff_lk2, diff_head_norm, diff_w_o, sconv_w_in, sconv_w, sconv_w_out, gmlp_w_in, gmlp_v_norm, gmlp_w_s, gmlp_b_s, gmlp_w_out, ffn_w_in, ffn_conv_w, ffn_conv_b, ffn_w_out):
    xp, xs = x_prompt.reshape(T_P, D), x_sample.reshape(T_S, D)
    cond =jnp.concatenate([c_ctx[None], c, jnp.zeros((MOD_ROWS - 1 - DEC_BATCH, D), F32)], axis=0)
    mods = _adaln(cond, ada_w, ada_b).reshape(DEPTH * MOD_ROWS, 1, N_MOD * D)
    cos_t, sin_t = _rope_tables()
    row = lambda a: a.reshape(1, -1)

    def ffn(y, hf, i):
        return _ffn(y, hf, mods, i, ffn_w_in[i].astype(BF16), ffn_conv_w[i], row(ffn_conv_b[i]),
                    ffn_w_out[i].astype(BF16), split_out=(i == DEPTH - 1))

    wd = _pad_lanes(mla_w_down[0], MLA_DOWN_W).astype(BF16)
    wuq = _pad_lanes(mla_w_uq[0].reshape(MLA_Q_LORA, MLA_HEADS, MLA_NOPE + MLA_ROPE), MLA_DK)
    wuq = wuq.reshape(MLA_Q_LORA, MLA_HEADS * MLA_DK).astype(BF16)
    q, ckv, kr = _mla_in(xp, xs, mods, 0, row(norm1_g[0]), wd, row(mla_q_norm[0]), row(mla_kv_norm[0]),
                         _pad_lanes(row(mla_kn_rope[0])), wuq, row(mla_qn_nope[0]),
                         _pad_lanes(row(mla_qn_rope[0])), cos_t, sin_t)
    state_ckv = ckv[:T_P].reshape(BATCH, 1, SEQ, MLA_KV_LORA)
    state_kr = kr[:T_P, :MLA_ROPE].reshape(BATCH, 1, SEQ, MLA_ROPE)
    k, v = _mla_kv(_kv_rows(cache_mla_ckv[:, 0], ckv), _kv_rows(_pad_lanes(cache_mla_krope[:, 0]), kr),
                   mla_w_uk[0].astype(BF16), mla_w_uv[0].astype(BF16), row(mla_kn_nope[0]))
    o = _attention(_mla_attn_kernel, "mla_attn", q, k, v, [], MLA_HEADS * MLA_DK, MLA_HEADS * MLA_V)
    y, hf = _mix_out((xp, xs), o, mods, 0, mla_w_o[0].astype(BF16), row(norm2_g[0]))
    x, = ffn(y, hf, 0)

    lam_init = 0.8 - 0.6 * math.exp(-0.3 * 1)
    wq, wk, wv = jnp.split(diff_w_qkv[0], 3, axis=-1)
    perm = lambda w: _heads_first(w.reshape(D, 2, DIFF_HEADS, DIFF_HD))
    w_qkv = jnp.concatenate([perm(wq), perm(wk), wv], axis=-1).astype(BF16)
    pair = lambda g: row(jnp.tile(g, 2))
    q, k, v, kst, vst = _diff_in(x, mods, 1, row(norm1_g[1]), w_qkv, pair(diff_qn[0]), pair(diff_kn[0]),
                                 cos_t, sin_t)
    state_dk = jnp.swapaxes(kst.reshape(BATCH, SEQ, DIFF_HEADS, 2, DIFF_HD), 2, 3)[:, None]
    state_dv = vst.reshape(BATCH, 1, SEQ, DIFF_HEADS, 2 * DIFF_HD)
    k_all = _kv_rows(_heads_first(cache_diff_k[:, 0]).astype(BF16), k)
    v_all = _kv_rows(cache_diff_v[:, 0].reshape(DEC_BATCH, PAST, D).astype(BF16), v)
    smalls = [row(diff_lq1[0]), row(diff_lk1[0]), row(diff_lq2[0]), row(diff_lk2[0]),
              row(diff_head_norm[0])]
    o = _attention(functools.partial(_diff_attn_kernel, lam_init), "diff_attn", q, k_all, v_all,
                   smalls, D, D)
    y, hf = _mix_out((x,), o, mods, 1, diff_w_o[0].astype(BF16), row(norm2_g[1]))
    x, = ffn(y, hf, 1)

    y, hf = _sconv(x, mods, 2, row(norm1_g[2]), sconv_w_in[0].astype(BF16), sconv_w[0],
                   sconv_w_out[0].astype(BF16), row(norm2_g[2]))
    x, = ffn(y, hf, 2)

    y, hf = _gmlp(x, mods, 3, row(norm1_g[3]), gmlp_w_in[0].astype(BF16), row(gmlp_v_norm[0]),
                  gmlp_w_s[0].astype(BF16), gmlp_b_s[0].T, gmlp_w_out[0].astype(BF16), row(norm2_g[3]))
    yp, ys = ffn(y, hf, 3)

    return (yp.reshape(BATCH, SEQ, D), ys.reshape(DEC_BATCH, DEC_SEQ, D),
            state_ckv, state_kr, state_dk, state_dv)
```

```python
import functools
import math

import jax
import jax.numpy as jnp
from jax import lax
from jax.experimental import pallas as pl
from jax.experimental.pallas import tpu as pltpu

D = 1024
BATCH, SEQ = 16, 256
DEC_BATCH, DEC_SEQ = 8, 2048
PAST = 256
DEPTH = 4
GRID_W = 64
EPS = 1e-6
ROPE_THETA = 10000.0
N_MOD = 6
MLA_HEADS, MLA_NOPE, MLA_ROPE, MLA_V = 8, 128, 64, 128
MLA_Q_LORA, MLA_KV_LORA = 768, 256
DIFF_HEADS, DIFF_HD = 8, 64
GMLP_CHUNK, GMLP_GROUPS = 128, 8
FFN_HIDDEN = 2816

T_P = BATCH * SEQ
T_S = DEC_BATCH * DEC_SEQ
T = T_P + T_S
TM = 256
NB_P = T_P // TM
NB = T // TM
BPS = DEC_SEQ // TM
KV_LEN = PAST + DEC_SEQ
T_KV = DEC_BATCH * KV_LEN + T_P
MOD_ROWS = 16
LANE = 128
MLA_DK = 256
HC = 256
HALO_BF16 = 16
HALO_F32 = 8
VMEM_LIMIT = 56 * 1024 * 1024

F32 = jnp.float32
BF16 = jnp.bfloat16


def _params(n_axes=1, vmem=VMEM_LIMIT):
    return pltpu.CompilerParams(dimension_semantics=("arbitrary",) * n_axes,
                                vmem_limit_bytes=vmem)


def _dot(a, b):
    return jnp.dot(a, b, preferred_element_type=F32)


def _rms(x, g):
    ms = jnp.mean(x * x, axis=-1, keepdims=True)
    return (x * lax.rsqrt(ms + EPS)) * g


def _modulate(x, g, shift, scale):
    return _rms(x, g) * (1.0 + scale) + shift


def _sigmoid(x):
    return 1.0 / (1.0 + jnp.exp(-x))


def _mod_parts(mod_ref):
    m = mod_ref[0]
    return [m[:, k * D:(k + 1) * D] for k in range(N_MOD)]


def _full(shape):
    return pl.BlockSpec(shape, lambda *_: (0,) * len(shape))


def _row_spec(width, rows=TM):
    return pl.BlockSpec((rows, width), lambda i: (i, 0))


def _split_specs(width):
    return [pl.BlockSpec((TM, width), lambda i: (jnp.minimum(i, NB_P - 1), 0)),
            pl.BlockSpec((TM, width), lambda i: (jnp.maximum(i - NB_P, 0), 0))]


def _split_rows(p_ref, s_ref):
    return jnp.where(pl.program_id(0) < NB_P, p_ref[...], s_ref[...])


def _mod_spec(layer):
    def imap(i):
        j = jnp.maximum(i - NB_P, 0)
        return (layer * MOD_ROWS + jnp.where(i < NB_P, 0, 1 + j // BPS), 0, 0)
    return pl.BlockSpec((1, 1, N_MOD * D), imap)


def _rope_spec():
    def imap(i):
        j = jnp.maximum(i - NB_P, 0)
        return (jnp.where(i < NB_P, BPS, j % BPS), 0)
    return pl.BlockSpec((TM, LANE), imap)


def _seq_edges(i):
    row0 = i * TM
    seq_len = jnp.where(row0 < T_P, SEQ, DEC_SEQ)
    pos = row0 % seq_len
    return pos != 0, pos + TM != seq_len


def _group_rms(xb, gain, left):
    sq = xb * xb
    sl = jnp.sum(jnp.where(left, sq, 0.0), axis=-1, keepdims=True)
    sr = jnp.sum(jnp.where(left, 0.0, sq), axis=-1, keepdims=True)
    ms = jnp.where(left, sl, sr) * (1.0 / 64.0)
    return (xb * lax.rsqrt(ms + EPS)) * gain


def _rope128(xb, cos, sin, first):
    swapped = jnp.where(first, pltpu.roll(xb, LANE - 32, 1), pltpu.roll(xb, 32, 1))
    return xb * cos + swapped * sin


ADA_TN = 1536


def _adaln_kernel(cond_ref, w_ref, b_ref, o_ref):
    c = cond_ref[...]
    a = (c * _sigmoid(c)).astype(BF16)
    o_ref[0] = _dot(a, w_ref[0].astype(BF16)) + b_ref[0]


def _adaln(cond, ada_w, ada_b):
    n = N_MOD * D
    return pl.pallas_call(
        _adaln_kernel,
        out_shape=jax.ShapeDtypeStruct((DEPTH, MOD_ROWS, n), F32),
        grid=(DEPTH, n // ADA_TN),
        in_specs=[pl.BlockSpec((MOD_ROWS, D), lambda l, j: (0, 0)),
                  pl.BlockSpec((1, D, ADA_TN), lambda l, j: (l, 0, j)),
                  pl.BlockSpec((1, 1, ADA_TN), lambda l, j: (l, 0, j))],
        out_specs=pl.BlockSpec((1, MOD_ROWS, ADA_TN), lambda l, j: (l, 0, j)),
        compiler_params=_params(2),
        name="adaln",
    )(cond, ada_w, ada_b.reshape(DEPTH, 1, n))


MLA_DOWN_W = MLA_Q_LORA + MLA_KV_LORA + LANE


def _mla_in_kernel(xp_ref, xs_ref, mod_ref, g1_ref, wd_ref, qn_ref, kvn_ref, knr_ref, wuq_ref,
                   qnn_ref, qnr_ref, cos_ref, sin_ref, q_ref, ckv_ref, kr_ref):
    shift, scale = _mod_parts(mod_ref)[:2]
    h = _modulate(_split_rows(xp_ref, xs_ref), g1_ref[...], shift, scale).astype(BF16)
    d = _dot(h, wd_ref[...])
    cq = _rms(d[:, :MLA_Q_LORA], qn_ref[...]).astype(BF16)
    ckv_ref[...] = _rms(d[:, MLA_Q_LORA:MLA_Q_LORA + MLA_KV_LORA], kvn_ref[...])

    lane = lax.broadcasted_iota(jnp.int32, (1, LANE), 1)
    first = (lane % 64) < 32
    cos, sin = cos_ref[...], sin_ref[...]

    def rope_part(xb, gain):
        ms = jnp.sum(xb * xb, axis=-1, keepdims=True) * (1.0 / MLA_ROPE)
        return _rope128((xb * lax.rsqrt(ms + EPS)) * gain, cos, sin, first)

    kr_ref[...] = rope_part(d[:, MLA_Q_LORA + MLA_KV_LORA:], knr_ref[...])

    q = _dot(cq, wuq_ref[...])
    for hh in range(MLA_HEADS):
        c0 = hh * MLA_DK
        q_ref[:, c0:c0 + LANE] = _rms(q[:, c0:c0 + LANE], qnn_ref[...]).astype(BF16)
        q_ref[:, c0 + LANE:c0 + MLA_DK] = rope_part(q[:, c0 + LANE:c0 + MLA_DK],
                                                    qnr_ref[...]).astype(BF16)


def _mla_in(xp, xs, mods, layer, g1, wd, qn, kvn, knr, wuq, qnn, qnr, cos_t, sin_t):
    return pl.pallas_call(
        _mla_in_kernel,
        out_shape=(jax.ShapeDtypeStruct((T, MLA_HEADS * MLA_DK), BF16),
                   jax.ShapeDtypeStruct((T, MLA_KV_LORA), F32),
                   jax.ShapeDtypeStruct((T, LANE), F32)),
        grid=(NB,),
        in_specs=[*_split_specs(D), _mod_spec(layer), _full((1, D)), _full((D, MLA_DOWN_W)),
                  _full((1, MLA_Q_LORA)), _full((1, MLA_KV_LORA)), _full((1, LANE)),
                  _full((MLA_Q_LORA, MLA_HEADS * MLA_DK)), _full((1, LANE)), _full((1, LANE)),
                  _rope_spec(), _rope_spec()],
        out_specs=(_row_spec(MLA_HEADS * MLA_DK), _row_spec(MLA_KV_LORA), _row_spec(LANE)),
        compiler_params=_params(),
        name="mla_in",
    )(xp, xs, mods, g1, wd, qn, kvn, knr, wuq, qnn, qnr, cos_t, sin_t)


def _mla_kv_kernel(ckv_ref, kr_ref, wuk_ref, wuv_ref, knn_ref, k_ref, v_ref):
    c = ckv_ref[...].astype(BF16)
    kn = _dot(c, wuk_ref[...])
    v_ref[...] = _dot(c, wuv_ref[...]).astype(BF16)
    kr = kr_ref[...].astype(BF16)
    for hh in range(MLA_HEADS):
        c0 = hh * MLA_DK
        k_ref[:, c0:c0 + LANE] = _rms(kn[:, hh * LANE:(hh + 1) * LANE], knn_ref[...]).astype(BF16)
        k_ref[:, c0 + LANE:c0 + MLA_DK] = kr


def _mla_kv(ckv_all, kr_all, wuk, wuv, knn):
    return pl.pallas_call(
        _mla_kv_kernel,
        out_shape=(jax.ShapeDtypeStruct((T_KV, MLA_HEADS * MLA_DK), BF16),
                   jax.ShapeDtypeStruct((T_KV, MLA_HEADS * MLA_V), BF16)),
        grid=(T_KV // TM,),
        in_specs=[_row_spec(MLA_KV_LORA), _row_spec(LANE),
                  _full((MLA_KV_LORA, MLA_HEADS * MLA_NOPE)), _full((MLA_KV_LORA, MLA_HEADS * MLA_V)),
                  _full((1, LANE))],
        out_specs=(_row_spec(MLA_HEADS * MLA_DK), _row_spec(MLA_HEADS * MLA_V)),
        compiler_params=_params(),
        name="mla_kv",
    )(ckv_all, kr_all, wuk, wuv, knn)


LOG2E = math.log2(math.e)


def _softmax_pv(s2, v_ones):
    p = jnp.exp2(s2 - jnp.max(s2, axis=-1, keepdims=True))
    o = _dot(p.astype(BF16), v_ones)
    return o[:, :LANE] * (1.0 / o[:, LANE:LANE + 1])


def _with_ones(v):
    return jnp.concatenate([v, jnp.ones_like(v)], axis=1)


def _mla_heads(q_ref, kt_ref, v_ref, o_ref):
    scale2 = (MLA_NOPE + MLA_ROPE) ** -0.5 * LOG2E
    for hh in range(MLA_HEADS):
        ks = slice(hh * MLA_DK, (hh + 1) * MLA_DK)
        vs = slice(hh * MLA_V, (hh + 1) * MLA_V)
        o = _softmax_pv(_dot(q_ref[:, ks], kt_ref[0, ks, :]) * scale2, _with_ones(v_ref[:, vs]))
        o_ref[:, vs] = o.astype(BF16)


def _diff_heads(lam_init, lq1_ref, lk1_ref, lq2_ref, lk2_ref, hn_ref, q_ref, kt_ref, v_ref, o_ref):
    lam = (jnp.exp(jnp.sum(lq1_ref[...] * lk1_ref[...], axis=-1, keepdims=True))
           - jnp.exp(jnp.sum(lq2_ref[...] * lk2_ref[...], axis=-1, keepdims=True)) + lam_init)
    lane = lax.broadcasted_iota(jnp.int32, (1, LANE), 1)
    left = lane < DIFF_HD
    zero = jnp.zeros((), BF16)
    for hh in range(DIFF_HEADS):
        cs = slice(hh * LANE, (hh + 1) * LANE)
        qb, kt, v1 = q_ref[:, cs], kt_ref[0, cs, :], _with_ones(v_ref[:, cs])
        o0 = _softmax_pv(_dot(jnp.where(left, qb, zero), kt) * LOG2E, v1)
        o1 = _softmax_pv(_dot(jnp.where(left, zero, qb), kt) * LOG2E, v1)
        o = o0 - lam * o1
        o_ref[:, cs] = (_rms(o, hn_ref[...]) * (1.0 - lam_init)).astype(BF16)


def _residual_and_ffn_in(x, mix, mods, g2):
    _, _, gate1, shift2, scale2, _ = mods
    y = x + gate1 * mix
    return y, _modulate(y, g2, shift2, scale2).astype(BF16)


def _attn_kernel(heads, n_x, n_small, *refs):
    x_refs, refs = refs[:n_x], refs[n_x:]
    q_ref, ktp_ref, ktl_ref, vp_ref, vl_ref = refs[:5]
    smalls, (mod_ref, wo_ref, g2_ref, y_ref, hf_ref, o_ref) = refs[5:5 + n_small], refs[5 + n_small:]
    is_prompt = pl.program_id(0) < NB_P

    @pl.when(is_prompt)
    def _():
        heads(*smalls, q_ref, ktp_ref, vp_ref, o_ref)

    @pl.when(jnp.logical_not(is_prompt))
    def _():
        heads(*smalls, q_ref, ktl_ref, vl_ref, o_ref)

    x = x_refs[0][...] if n_x == 1 else _split_rows(*x_refs)
    y, hf = _residual_and_ffn_in(x, _dot(o_ref[...], wo_ref[...]), _mod_parts(mod_ref), g2_ref[...])
    y_ref[...] = y
    hf_ref[...] = hf


def _attention(heads, name, xs, q, k, v, smalls, mods, layer, wo, g2, dk, dv):
    assert SEQ == TM
    n_lat = DEC_BATCH * KV_LEN
    kt_lat = jnp.swapaxes(k[:n_lat].reshape(DEC_BATCH, KV_LEN, dk), 1, 2)
    kt_prompt = jnp.swapaxes(k[n_lat:].reshape(BATCH, SEQ, dk), 1, 2)
    pro = lambda i: jnp.minimum(i, NB_P - 1)
    lat = lambda i: jnp.maximum(i - NB_P, 0) // BPS
    x_specs = [_row_spec(D)] if len(xs) == 1 else _split_specs(D)
    return pl.pallas_call(
        functools.partial(_attn_kernel, heads, len(xs), len(smalls)),
        out_shape=(jax.ShapeDtypeStruct((T, D), F32), jax.ShapeDtypeStruct((T, D), BF16)),
        grid=(NB,),
        in_specs=[*x_specs, _row_spec(dk),
                  pl.BlockSpec((1, dk, SEQ), lambda i: (pro(i), 0, 0)),
                  pl.BlockSpec((1, dk, KV_LEN), lambda i: (lat(i), 0, 0)),
                  pl.BlockSpec((SEQ, dv), lambda i: (n_lat // SEQ + pro(i), 0)),
                  pl.BlockSpec((KV_LEN, dv), lambda i: (lat(i), 0)),
                  *[_full(s.shape) for s in smalls], _mod_spec(layer), _full((dv, D)), _full((1, D))],
        out_specs=(_row_spec(D), _row_spec(D)),
        scratch_shapes=[pltpu.VMEM((TM, dv), BF16)],
        compiler_params=_params(),
        name=name,
    )(*xs, q, kt_prompt, kt_lat, v, v, *smalls, mods, wo, g2)


def _conv3(zs_ref, w, off):
    z = zs_ref[...]
    n = z.shape[0]
    z_prev = pltpu.roll(z, 1, 0)[off:off + TM]
    z_next = pltpu.roll(z, n - 1, 0)[off:off + TM]
    return z_prev * w[0:1] + z[off:off + TM] * w[1:2] + z_next * w[2:3]


def _ffn_kernel(n_out, y_ref, hf_ref, prev_ref, next_ref, mod_ref, win_ref, cw_ref, cb_ref, wout_ref,
                *rest):
    out_refs, (lhs_ref, z_ref, act_ref) = rest[:n_out], rest[n_out:]
    has_prev, has_next = _seq_edges(pl.program_id(0))
    hb = HALO_BF16
    zeros = jnp.zeros((hb, D), BF16)
    lhs_ref[0:hb, :] = jnp.where(has_prev, prev_ref[...], zeros)
    lhs_ref[hb:hb + TM, :] = hf_ref[...]
    lhs_ref[hb + TM:, :] = jnp.where(has_next, next_ref[...], zeros)
    lhs = lhs_ref[...]
    for c in range(FFN_HIDDEN // HC):
        gs = slice(c * HC, (c + 1) * HC)
        us = slice(FFN_HIDDEN + c * HC, FFN_HIDDEN + (c + 1) * HC)
        z_ref[:, gs] = _dot(lhs, win_ref[:, gs])
        z_ref[:, us] = _dot(lhs, win_ref[:, us])
        g = _conv3(z_ref.at[:, gs], cw_ref[:, gs], hb) + cb_ref[:, gs]
        u = _conv3(z_ref.at[:, us], cw_ref[:, us], hb) + cb_ref[:, us]
        act_ref[:, gs] = ((g * _sigmoid(g)) * u).astype(BF16)
    gate2 = _mod_parts(mod_ref)[5]
    out = y_ref[...] + gate2 * _dot(act_ref[...], wout_ref[...])
    if len(out_refs) == 1:
        out_refs[0][...] = out
    else:
        @pl.when(pl.program_id(0) < NB_P)
        def _():
            out_refs[0][...] = out

        @pl.when(pl.program_id(0) >= NB_P)
        def _():
            out_refs[1][...] = out


def _ffn(y, hf, mods, layer, win, cw, cb, wout, split_out=False):
    hb = HALO_BF16
    r = TM // hb
    if split_out:
        out_shape = (jax.ShapeDtypeStruct((T_P, D), F32), jax.ShapeDtypeStruct((T_S, D), F32))
        out_specs = tuple(_split_specs(D))
    else:
        out_shape, out_specs = (jax.ShapeDtypeStruct((T, D), F32),), (_row_spec(D),)
    return pl.pallas_call(
        functools.partial(_ffn_kernel, len(out_shape)),
        out_shape=out_shape,
        grid=(NB,),
        in_specs=[_row_spec(D), _row_spec(D),
                  pl.BlockSpec((hb, D), lambda i: (jnp.maximum(i * r - 1, 0), 0)),
                  pl.BlockSpec((hb, D), lambda i: (jnp.minimum((i + 1) * r, T // hb - 1), 0)),
                  _mod_spec(layer), _full((D, 2 * FFN_HIDDEN)), _full((3, 2 * FFN_HIDDEN)),
                  _full((1, 2 * FFN_HIDDEN)), _full((FFN_HIDDEN, D))],
        out_specs=out_specs,
        scratch_shapes=[pltpu.VMEM((TM + 2 * hb, D), BF16),
                        pltpu.VMEM((TM + 2 * hb, 2 * FFN_HIDDEN), F32),
                        pltpu.VMEM((TM, FFN_HIDDEN), BF16)],
        compiler_params=_params(),
        name="ffn",
    )(y, hf, hf, hf, mods, win, cw, cb, wout)


def _diff_in_kernel(x_ref, mod_ref, g1_ref, w_ref, qn_ref, kn_ref, cos_ref, sin_ref,
                    q_ref, k_ref, v_ref, kst_ref, vst_ref, kn_ref_scratch):
    shift, scale = _mod_parts(mod_ref)[:2]
    h = _modulate(x_ref[...], g1_ref[...], shift, scale).astype(BF16)
    z = _dot(h, w_ref[...])
    lane = lax.broadcasted_iota(jnp.int32, (1, LANE), 1)
    left = lane < DIFF_HD
    first = (lane % 64) < 32
    cos, sin = cos_ref[...], sin_ref[...]
    v = z[:, 2 * D:]
    v_ref[...] = v.astype(BF16)
    for hh in range(DIFF_HEADS):
        cs = slice(hh * LANE, (hh + 1) * LANE)
        qn = _group_rms(z[:, hh * LANE:(hh + 1) * LANE], qn_ref[...], left)
        q_ref[:, cs] = (_rope128(qn, cos, sin, first) * DIFF_HD ** -0.5).astype(BF16)
        kn = _group_rms(z[:, D + hh * LANE:D + (hh + 1) * LANE], kn_ref[...], left)
        k_ref[:, cs] = _rope128(kn, cos, sin, first).astype(BF16)
        kn_ref_scratch[:, cs] = kn

    @pl.when(pl.program_id(0) < NB_P)
    def _():
        vst_ref[...] = v
        kst_ref[...] = kn_ref_scratch[...]


def _diff_in(x, mods, layer, g1, w, qn, kn, cos_t, sin_t):
    st_spec = pl.BlockSpec((TM, D), lambda i: (jnp.minimum(i, NB_P - 1), 0))
    return pl.pallas_call(
        _diff_in_kernel,
        out_shape=(jax.ShapeDtypeStruct((T, D), BF16), jax.ShapeDtypeStruct((T, D), BF16),
                   jax.ShapeDtypeStruct((T, D), BF16), jax.ShapeDtypeStruct((T_P, D), F32),
                   jax.ShapeDtypeStruct((T_P, D), F32)),
        grid=(NB,),
        in_specs=[_row_spec(D), _mod_spec(layer), _full((1, D)), _full((D, 3 * D)),
                  _full((1, LANE)), _full((1, LANE)), _rope_spec(), _rope_spec()],
        out_specs=(_row_spec(D), _row_spec(D), _row_spec(D), st_spec, st_spec),
        scratch_shapes=[pltpu.VMEM((TM, D), F32)],
        compiler_params=_params(),
        name="diff_in",
    )(x, mods, g1, w, qn, kn, cos_t, sin_t)


def _sconv_kernel(x_ref, prev_ref, next_ref, mod_ref, g1_ref, win_ref, cw_ref, wout_ref, g2_ref,
                  y_ref, hf_ref, p_ref):
    has_prev, has_next = _seq_edges(pl.program_id(0))
    hb = HALO_F32
    mods = _mod_parts(mod_ref)
    x = x_ref[...]
    mod_in = lambda a: _modulate(a, g1_ref[...], mods[0], mods[1])
    zeros = jnp.zeros((hb, D), F32)
    lhs = jnp.concatenate([jnp.where(has_prev, mod_in(prev_ref[...]), zeros), mod_in(x),
                           jnp.where(has_next, mod_in(next_ref[...]), zeros)],
                          axis=0).astype(BF16)
    acc = jnp.zeros((TM, D), F32)
    for c in range(D // HC):
        cs = slice(c * HC, (c + 1) * HC)
        gb = _dot(lhs, win_ref[:, c * HC:(c + 1) * HC])
        gc = _dot(lhs, win_ref[:, D + c * HC:D + (c + 1) * HC])
        u = _dot(lhs, win_ref[:, 2 * D + c * HC:2 * D + (c + 1) * HC])
        p_ref[...] = gc * u
        mixed = gb[hb:hb + TM] * _conv3(p_ref, cw_ref[:, cs], hb)
        acc = acc + _dot(mixed.astype(BF16), wout_ref[cs, :])
    y, hf = _residual_and_ffn_in(x, acc, mods, g2_ref[...])
    y_ref[...] = y
    hf_ref[...] = hf


def _sconv(x, mods, layer, g1, win, cw, wout, g2):
    hb = HALO_F32
    r = TM // hb
    return pl.pallas_call(
        _sconv_kernel,
        out_shape=(jax.ShapeDtypeStruct((T, D), F32), jax.ShapeDtypeStruct((T, D), BF16)),
        grid=(NB,),
        in_specs=[_row_spec(D),
                  pl.BlockSpec((hb, D), lambda i: (jnp.maximum(i * r - 1, 0), 0)),
                  pl.BlockSpec((hb, D), lambda i: (jnp.minimum((i + 1) * r, T // hb - 1), 0)),
                  _mod_spec(layer), _full((1, D)), _full((D, 3 * D)), _full((3, D)),
                  _full((D, D)), _full((1, D))],
        out_specs=(_row_spec(D), _row_spec(D)),
        scratch_shapes=[pltpu.VMEM((TM + 2 * hb, HC), F32)],
        compiler_params=_params(),
        name="sconv",
    )(x, x, x, mods, g1, win, cw, wout, g2)


def _gmlp_kernel(x_ref, mod_ref, g1_ref, win_ref, vn_ref, ws_ref, bs_ref, wout_ref, g2_ref,
                 y_ref, hf_ref, gated_ref):
    mods = _mod_parts(mod_ref)
    x = x_ref[...]
    h = _modulate(x, g1_ref[...], mods[0], mods[1]).astype(BF16)
    z = _dot(h, win_ref[...])
    z = z * (0.5 * (1.0 + jnp.tanh(math.sqrt(2.0 / math.pi) * (z + 0.044715 * (z * z * z)))))
    u = z[:, :D]
    v = _rms(z[:, D:], vn_ref[...]).astype(BF16)
    bs = bs_ref[...]
    for g in range(GMLP_GROUPS):
        cs = slice(g * LANE, (g + 1) * LANE)
        w = ws_ref[g]
        for r in range(TM // GMLP_CHUNK):
            rs = slice(r * GMLP_CHUNK, (r + 1) * GMLP_CHUNK)
            mixed = _dot(w, v[rs, cs]) + bs[:, g:g + 1]
            gated_ref[rs, cs] = (u[rs, cs] * mixed).astype(BF16)
    y, hf = _residual_and_ffn_in(x, _dot(gated_ref[...], wout_ref[...]), mods, g2_ref[...])
    y_ref[...] = y
    hf_ref[...] = hf


def _gmlp(x, mods, layer, g1, win, vn, ws, bs_t, wout, g2):
    return pl.pallas_call(
        _gmlp_kernel,
        out_shape=(jax.ShapeDtypeStruct((T, D), F32), jax.ShapeDtypeStruct((T, D), BF16)),
        grid=(NB,),
        in_specs=[_row_spec(D), _mod_spec(layer), _full((1, D)), _full((D, 2 * D)), _full((1, D)),
                  _full((GMLP_GROUPS, GMLP_CHUNK, GMLP_CHUNK)), _full((GMLP_CHUNK, GMLP_GROUPS)),
                  _full((D, D)), _full((1, D))],
        out_specs=(_row_spec(D), _row_spec(D)),
        scratch_shapes=[pltpu.VMEM((TM, D), BF16)],
        compiler_params=_params(),
        name="gmlp",
    )(x, mods, g1, win, vn, ws, bs_t, wout, g2)


def _rope_tables():
    rows = DEC_SEQ // GRID_W
    row = jnp.repeat(jnp.arange(rows, dtype=F32), GRID_W)
    col = jnp.tile(jnp.arange(GRID_W, dtype=F32), rows)
    n_freq = MLA_ROPE // 4
    inv_freq = ROPE_THETA ** (-jnp.arange(n_freq, dtype=F32) / n_freq)
    ang = jnp.concatenate([row[:, None] * inv_freq, col[:, None] * inv_freq], axis=-1)
    cos, sin = jnp.cos(ang), jnp.sin(ang)
    cos = jnp.concatenate([cos, jnp.ones((TM, cos.shape[1]), F32)], axis=0)
    sin = jnp.concatenate([sin, jnp.zeros((TM, sin.shape[1]), F32)], axis=0)
    return jnp.tile(cos, (1, 4)), jnp.tile(jnp.concatenate([-sin, sin], axis=1), (1, 2))


def _pad_lanes(a, width=LANE):
    return jnp.pad(a, [(0, 0)] * (a.ndim - 1) + [(0, width - a.shape[-1])])


def _kv_rows(cache, new):
    w = new.shape[-1]
    lat = jnp.concatenate([cache, new[T_P:].reshape(DEC_BATCH, DEC_SEQ, w)], axis=1)
    return jnp.concatenate([lat.reshape(DEC_BATCH * KV_LEN, w), new[:T_P]], axis=0)


def _heads_first(a):
    return jnp.swapaxes(a, -3, -2).reshape(a.shape[:-3] + (D,))


def kernel(x_prompt, x_sample, cache_mla_ckv, cache_mla_krope, cache_diff_k, cache_diff_v, c, c_ctx, ada_w, ada_b, norm1_g, norm2_g, mla_w_down, mla_q_norm, mla_kv_norm, mla_w_uq, mla_w_uk, mla_w_uv, mla_qn_nope, mla_qn_rope, mla_kn_nope, mla_kn_rope, mla_w_o, diff_w_qkv, diff_qn, diff_kn, diff_lq1, diff_lk1, diff_lq2, diff_lk2, diff_head_norm, diff_w_o, sconv_w_in, sconv_w, sconv_w_out, gmlp_w_in, gmlp_v_norm, gmlp_w_s, gmlp_b_s, gmlp_w_out, ffn_w_in, ffn_conv_w, ffn_conv_b, ffn_w_out):
    xp, xs = x_prompt.reshape(T_P, D), x_sample.reshape(T_S, D)
    cond =jnp.concatenate([c_ctx[None], c, jnp.zeros((MOD_ROWS - 1 - DEC_BATCH, D), F32)], axis=0)
    mods = _adaln(cond, ada_w, ada_b).reshape(DEPTH * MOD_ROWS, 1, N_MOD * D)
    cos_t, sin_t = _rope_tables()
    row = lambda a: a.reshape(1, -1)

    def ffn(y, hf, i):
        return _ffn(y, hf, mods, i, ffn_w_in[i].astype(BF16), ffn_conv_w[i], row(ffn_conv_b[i]),
                    ffn_w_out[i].astype(BF16), split_out=(i == DEPTH - 1))

    wd = _pad_lanes(mla_w_down[0], MLA_DOWN_W).astype(BF16)
    wuq = _pad_lanes(mla_w_uq[0].reshape(MLA_Q_LORA, MLA_HEADS, MLA_NOPE + MLA_ROPE), MLA_DK)
    wuq = wuq.reshape(MLA_Q_LORA, MLA_HEADS * MLA_DK).astype(BF16)
    q, ckv, kr = _mla_in(xp, xs, mods, 0, row(norm1_g[0]), wd, row(mla_q_norm[0]), row(mla_kv_norm[0]),
                         _pad_lanes(row(mla_kn_rope[0])), wuq, row(mla_qn_nope[0]),
                         _pad_lanes(row(mla_qn_rope[0])), cos_t, sin_t)
    state_ckv = ckv[:T_P].reshape(BATCH, 1, SEQ, MLA_KV_LORA)
    state_kr = kr[:T_P, :MLA_ROPE].reshape(BATCH, 1, SEQ, MLA_ROPE)
    k, v = _mla_kv(_kv_rows(cache_mla_ckv[:, 0], ckv), _kv_rows(_pad_lanes(cache_mla_krope[:, 0]), kr),
                   mla_w_uk[0].astype(BF16), mla_w_uv[0].astype(BF16), row(mla_kn_nope[0]))
    y, hf = _attention(_mla_heads, "mla_attn", (xp, xs), q, k, v, [], mods, 0, mla_w_o[0].astype(BF16),
                       row(norm2_g[0]), MLA_HEADS * MLA_DK, MLA_HEADS * MLA_V)
    x, = ffn(y, hf, 0)

    lam_init = 0.8 - 0.6 * math.exp(-0.3 * 1)
    wq, wk, wv = jnp.split(diff_w_qkv[0], 3, axis=-1)
    perm = lambda w: _heads_first(w.reshape(D, 2, DIFF_HEADS, DIFF_HD))
    w_qkv = jnp.concatenate([perm(wq), perm(wk), wv], axis=-1).astype(BF16)
    pair = lambda g: row(jnp.tile(g, 2))
    q, k, v, kst, vst = _diff_in(x, mods, 1, row(norm1_g[1]), w_qkv, pair(diff_qn[0]), pair(diff_kn[0]),
                                 cos_t, sin_t)
    state_dk = jnp.swapaxes(kst.reshape(BATCH, SEQ, DIFF_HEADS, 2, DIFF_HD), 2, 3)[:, None]
    state_dv = vst.reshape(BATCH, 1, SEQ, DIFF_HEADS, 2 * DIFF_HD)
    k_all = _kv_rows(_heads_first(cache_diff_k[:, 0]).astype(BF16), k)
    v_all = _kv_rows(cache_diff_v[:, 0].reshape(DEC_BATCH, PAST, D).astype(BF16), v)
    smalls = [row(diff_lq1[0]), row(diff_lk1[0]), row(diff_lq2[0]), row(diff_lk2[0]),
              row(diff_head_norm[0])]
    y, hf = _attention(functools.partial(_diff_heads, lam_init), "diff_attn", (x,), q, k_all, v_all,
                       smalls, mods, 1, diff_w_o[0].astype(BF16), row(norm2_g[1]), D, D)
    x, = ffn(y, hf, 1)

    y, hf = _sconv(x, mods, 2, row(norm1_g[2]), sconv_w_in[0].astype(BF16), sconv_w[0],
                   sconv_w_out[0].astype(BF16), row(norm2_g[2]))
    x, = ffn(y, hf, 2)

    y, hf = _gmlp(x, mods, 3, row(norm1_g[3]), gmlp_w_in[0].astype(BF16), row(gmlp_v_norm[0]),
                  gmlp_w_s[0].astype(BF16), gmlp_b_s[0].T, gmlp_w_out[0].astype(BF16), row(norm2_g[3]))
    yp, ys = ffn(y, hf, 3)

    return (yp.reshape(BATCH, SEQ, D), ys.reshape(DEC_BATCH, DEC_SEQ, D),
            state_ckv, state_kr, state_dk, state_dv)
```

```python
import functools
import math

import jax
import jax.numpy as jnp
from jax import lax
from jax.experimental import pallas as pl
from jax.experimental.pallas import tpu as pltpu

D = 1024
BATCH, SEQ = 16, 256
DEC_BATCH, DEC_SEQ = 8, 2048
PAST = 256
DEPTH = 4
GRID_W = 64
EPS = 1e-6
ROPE_THETA = 10000.0
N_MOD = 6
MLA_HEADS, MLA_NOPE, MLA_ROPE, MLA_V = 8, 128, 64, 128
MLA_Q_LORA, MLA_KV_LORA = 768, 256
DIFF_HEADS, DIFF_HD = 8, 64
GMLP_CHUNK, GMLP_GROUPS = 128, 8
FFN_HIDDEN = 2816

T_P = BATCH * SEQ
T_S = DEC_BATCH * DEC_SEQ
T = T_P + T_S
TM = 256
NB_P = T_P // TM
NB = T // TM
BPS = DEC_SEQ // TM
KV_LEN = PAST + DEC_SEQ
MOD_ROWS = 16
LANE = 128
MLA_DK = 256
HC = 256
HALO_BF16 = 16
HALO_F32 = 8
VMEM_LIMIT = 56 * 1024 * 1024

F32 = jnp.float32
BF16 = jnp.bfloat16


def _params(n_axes=1, vmem=VMEM_LIMIT):
    return pltpu.CompilerParams(dimension_semantics=("arbitrary",) * n_axes,
                                vmem_limit_bytes=vmem)


def _dot(a, b):
    return jnp.dot(a, b, preferred_element_type=F32)


def _rms(x, g):
    ms = jnp.mean(x * x, axis=-1, keepdims=True)
    return (x * lax.rsqrt(ms + EPS)) * g


def _modulate(x, g, shift, scale):
    return _rms(x, g) * (1.0 + scale) + shift


def _sigmoid(x):
    return 1.0 / (1.0 + jnp.exp(-x))


def _mod_parts(mod_ref):
    m = mod_ref[0]
    return [m[:, k * D:(k + 1) * D] for k in range(N_MOD)]


def _full(shape):
    return pl.BlockSpec(shape, lambda *_: (0,) * len(shape))


def _row_spec(width, rows=TM):
    return pl.BlockSpec((rows, width), lambda i: (i, 0))


def _split_specs(width):
    return [pl.BlockSpec((TM, width), lambda i: (jnp.minimum(i, NB_P - 1), 0)),
            pl.BlockSpec((TM, width), lambda i: (jnp.maximum(i - NB_P, 0), 0))]


def _split_rows(p_ref, s_ref):
    return jnp.where(pl.program_id(0) < NB_P, p_ref[...], s_ref[...])


def _mod_spec(layer):
    def imap(i):
        j = jnp.maximum(i - NB_P, 0)
        return (layer * MOD_ROWS + jnp.where(i < NB_P, 0, 1 + j // BPS), 0, 0)
    return pl.BlockSpec((1, 1, N_MOD * D), imap)


def _rope_spec():
    def imap(i):
        j = jnp.maximum(i - NB_P, 0)
        return (jnp.where(i < NB_P, BPS, j % BPS), 0)
    return pl.BlockSpec((TM, LANE), imap)


def _seq_edges(i):
    row0 = i * TM
    seq_len = jnp.where(row0 < T_P, SEQ, DEC_SEQ)
    pos = row0 % seq_len
    return pos != 0, pos + TM != seq_len


def _group_rms(xb, gain, left):
    sq = xb * xb
    sl = jnp.sum(jnp.where(left, sq, 0.0), axis=-1, keepdims=True)
    sr = jnp.sum(jnp.where(left, 0.0, sq), axis=-1, keepdims=True)
    ms = jnp.where(left, sl, sr) * (1.0 / 64.0)
    return (xb * lax.rsqrt(ms + EPS)) * gain


def _rope128(xb, cos, sin, first):
    swapped = jnp.where(first, pltpu.roll(xb, LANE - 32, 1), pltpu.roll(xb, 32, 1))
    return xb * cos + swapped * sin


ADA_TN = 1536


def _adaln_kernel(cond_ref, w_ref, b_ref, o_ref):
    c = cond_ref[...]
    a = (c * _sigmoid(c)).astype(BF16)
    o_ref[0] = _dot(a, w_ref[0].astype(BF16)) + b_ref[0]


def _adaln(cond, ada_w, ada_b):
    n = N_MOD * D
    return pl.pallas_call(
        _adaln_kernel,
        out_shape=jax.ShapeDtypeStruct((DEPTH, MOD_ROWS, n), F32),
        grid=(DEPTH, n // ADA_TN),
        in_specs=[pl.BlockSpec((MOD_ROWS, D), lambda l, j: (0, 0)),
                  pl.BlockSpec((1, D, ADA_TN), lambda l, j: (l, 0, j)),
                  pl.BlockSpec((1, 1, ADA_TN), lambda l, j: (l, 0, j))],
        out_specs=pl.BlockSpec((1, MOD_ROWS, ADA_TN), lambda l, j: (l, 0, j)),
        compiler_params=_params(2),
        name="adaln",
    )(cond, ada_w, ada_b.reshape(DEPTH, 1, n))


MLA_DOWN_W = MLA_Q_LORA + MLA_KV_LORA + LANE


def _mla_in_kernel(xp_ref, xs_ref, mod_ref, g1_ref, wd_ref, qn_ref, kvn_ref, knr_ref, wuq_ref,
                   qnn_ref, qnr_ref, cos_ref, sin_ref, q_ref, ckv_ref, kr_ref):
    shift, scale = _mod_parts(mod_ref)[:2]
    h = _modulate(_split_rows(xp_ref, xs_ref), g1_ref[...], shift, scale).astype(BF16)
    d = _dot(h, wd_ref[...])
    cq = _rms(d[:, :MLA_Q_LORA], qn_ref[...]).astype(BF16)
    ckv_ref[...] = _rms(d[:, MLA_Q_LORA:MLA_Q_LORA + MLA_KV_LORA], kvn_ref[...])

    lane = lax.broadcasted_iota(jnp.int32, (1, LANE), 1)
    first = (lane % 64) < 32
    cos, sin = cos_ref[...], sin_ref[...]

    def rope_part(xb, gain):
        ms = jnp.sum(xb * xb, axis=-1, keepdims=True) * (1.0 / MLA_ROPE)
        return _rope128((xb * lax.rsqrt(ms + EPS)) * gain, cos, sin, first)

    kr_ref[...] = rope_part(d[:, MLA_Q_LORA + MLA_KV_LORA:], knr_ref[...])

    q = _dot(cq, wuq_ref[...])
    for hh in range(MLA_HEADS):
        c0 = hh * MLA_DK
        q_ref[:, c0:c0 + LANE] = _rms(q[:, c0:c0 + LANE], qnn_ref[...]).astype(BF16)
        q_ref[:, c0 + LANE:c0 + MLA_DK] = rope_part(q[:, c0 + LANE:c0 + MLA_DK],
                                                    qnr_ref[...]).astype(BF16)


def _mla_in(xp, xs, mods, layer, g1, wd, qn, kvn, knr, wuq, qnn, qnr, cos_t, sin_t):
    return pl.pallas_call(
        _mla_in_kernel,
        out_shape=(jax.ShapeDtypeStruct((T, MLA_HEADS * MLA_DK), BF16),
                   jax.ShapeDtypeStruct((T, MLA_KV_LORA), F32),
                   jax.ShapeDtypeStruct((T, LANE), F32)),
        grid=(NB,),
        in_specs=[*_split_specs(D), _mod_spec(layer), _full((1, D)), _full((D, MLA_DOWN_W)),
                  _full((1, MLA_Q_LORA)), _full((1, MLA_KV_LORA)), _full((1, LANE)),
                  _full((MLA_Q_LORA, MLA_HEADS * MLA_DK)), _full((1, LANE)), _full((1, LANE)),
                  _rope_spec(), _rope_spec()],
        out_specs=(_row_spec(MLA_HEADS * MLA_DK), _row_spec(MLA_KV_LORA), _row_spec(LANE)),
        compiler_params=_params(),
        name="mla_in",
    )(xp, xs, mods, g1, wd, qn, kvn, knr, wuq, qnn, qnr, cos_t, sin_t)


def _mla_kv_kernel(ckv_ref, kr_ref, wuk_ref, wuv_ref, knn_ref, kt_ref, v_ref):
    c = ckv_ref[...].astype(BF16)
    kn = _dot(c, wuk_ref[...])
    v_ref[...] = _dot(c, wuv_ref[...]).astype(BF16)
    kr_t = kr_ref[...].T.astype(BF16)
    for hh in range(MLA_HEADS):
        r0 = hh * MLA_DK
        kh = _rms(kn[:, hh * LANE:(hh + 1) * LANE], knn_ref[...])
        kt_ref[r0:r0 + LANE, :] = kh.T.astype(BF16)
        kt_ref[r0 + LANE:r0 + MLA_DK, :] = kr_t


def _mla_kv(ckv, kr, wuk, wuv, knn):
    n = ckv.shape[0]
    return pl.pallas_call(
        _mla_kv_kernel,
        out_shape=(jax.ShapeDtypeStruct((MLA_HEADS * MLA_DK, n), BF16),
                   jax.ShapeDtypeStruct((n, MLA_HEADS * MLA_V), BF16)),
        grid=(n // TM,),
        in_specs=[_row_spec(MLA_KV_LORA), _row_spec(LANE),
                  _full((MLA_KV_LORA, MLA_HEADS * MLA_NOPE)), _full((MLA_KV_LORA, MLA_HEADS * MLA_V)),
                  _full((1, LANE))],
        out_specs=(pl.BlockSpec((MLA_HEADS * MLA_DK, TM), lambda i: (0, i)), _row_spec(MLA_HEADS * MLA_V)),
        compiler_params=_params(),
        name="mla_kv",
    )(ckv, kr, wuk, wuv, knn)


LOG2E = math.log2(math.e)


def _scores(q, kts, rows):
    parts = [_dot(q, kt_ref[rows, :]) for kt_ref in kts]
    return parts[0] if len(parts) == 1 else jnp.concatenate(parts, axis=1)


def _softmax_pv(s2, vs, cols):
    p = jnp.exp2(s2 - jnp.max(s2, axis=-1, keepdims=True)).astype(BF16)
    o, k0 = None, 0
    for v_ref in vs:
        v = v_ref[:, cols]
        part = _dot(p[:, k0:k0 + v.shape[0]], jnp.concatenate([v, jnp.ones_like(v)], axis=1))
        o = part if o is None else o + part
        k0 += v.shape[0]
    return o[:, :LANE] * (1.0 / o[:, LANE:LANE + 1])


def _mla_heads(q_ref, kts, vs, o_ref):
    scale2 = (MLA_NOPE + MLA_ROPE) ** -0.5 * LOG2E
    for hh in range(MLA_HEADS):
        ks = slice(hh * MLA_DK, (hh + 1) * MLA_DK)
        cs = slice(hh * MLA_V, (hh + 1) * MLA_V)
        o_ref[:, cs] = _softmax_pv(_scores(q_ref[:, ks], kts, ks) * scale2, vs, cs).astype(BF16)


def _diff_heads(lam_init, lq1_ref, lk1_ref, lq2_ref, lk2_ref, hn_ref, q_ref, kts, vs, o_ref):
    lam = (jnp.exp(jnp.sum(lq1_ref[...] * lk1_ref[...], axis=-1, keepdims=True))
           - jnp.exp(jnp.sum(lq2_ref[...] * lk2_ref[...], axis=-1, keepdims=True)) + lam_init)
    lane = lax.broadcasted_iota(jnp.int32, (1, LANE), 1)
    left = lane < DIFF_HD
    zero = jnp.zeros((), BF16)
    for hh in range(DIFF_HEADS):
        cs = slice(hh * LANE, (hh + 1) * LANE)
        qb = q_ref[:, cs]
        o0 = _softmax_pv(_scores(jnp.where(left, qb, zero), kts, cs) * LOG2E, vs, cs)
        o1 = _softmax_pv(_scores(jnp.where(left, zero, qb), kts, cs) * LOG2E, vs, cs)
        o = o0 - lam * o1
        o_ref[:, cs] = (_rms(o, hn_ref[...]) * (1.0 - lam_init)).astype(BF16)


def _residual_and_ffn_in(x, mix, mods, g2):
    _, _, gate1, shift2, scale2, _ = mods
    y = x + gate1 * mix
    return y, _modulate(y, g2, shift2, scale2).astype(BF16)


def _attn_kernel(heads, n_x, n_small, *refs):
    x_refs, refs = refs[:n_x], refs[n_x:]
    q_ref, ktp_ref, ktn_ref, ktc_ref, vp_ref, vn_ref, vc_ref = refs[:7]
    smalls, (mod_ref, wo_ref, g2_ref, y_ref, hf_ref, o_ref) = refs[7:7 + n_small], refs[7 + n_small:]
    is_prompt = pl.program_id(0) < NB_P

    @pl.when(is_prompt)
    def _():
        heads(*smalls, q_ref, [ktp_ref], [vp_ref], o_ref)

    @pl.when(jnp.logical_not(is_prompt))
    def _():
        heads(*smalls, q_ref, [ktc_ref, ktn_ref], [vc_ref, vn_ref], o_ref)

    x = x_refs[0][...] if n_x == 1 else _split_rows(*x_refs)
    y, hf = _residual_and_ffn_in(x, _dot(o_ref[...], wo_ref[...]), _mod_parts(mod_ref), g2_ref[...])
    y_ref[...] = y
    hf_ref[...] = hf


def _attention(heads, name, xs, q, kt, v, kt_cache, v_cache, smalls, mods, layer, wo, g2, dk, dv):
    assert SEQ == TM and T_P % DEC_SEQ == 0
    pro = lambda i: jnp.minimum(i, NB_P - 1)
    lat = lambda i: jnp.maximum(i - NB_P, 0) // BPS
    lat_new = lambda i: T_P // DEC_SEQ + lat(i)
    x_specs = [_row_spec(D)] if len(xs) == 1 else _split_specs(D)
    return pl.pallas_call(
        functools.partial(_attn_kernel, heads, len(xs), len(smalls)),
        out_shape=(jax.ShapeDtypeStruct((T, D), F32), jax.ShapeDtypeStruct((T, D), BF16)),
        grid=(NB,),
        in_specs=[*x_specs, _row_spec(dk),
                  pl.BlockSpec((dk, SEQ), lambda i: (0, pro(i))),
                  pl.BlockSpec((dk, DEC_SEQ), lambda i: (0, lat_new(i))),
                  pl.BlockSpec((dk, PAST), lambda i: (0, lat(i))),
                  pl.BlockSpec((SEQ, dv), lambda i: (pro(i), 0)),
                  pl.BlockSpec((DEC_SEQ, dv), lambda i: (lat_new(i), 0)),
                  pl.BlockSpec((PAST, dv), lambda i: (lat(i), 0)),
                  *[_full(s.shape) for s in smalls], _mod_spec(layer), _full((dv, D)), _full((1, D))],
        out_specs=(_row_spec(D), _row_spec(D)),
        scratch_shapes=[pltpu.VMEM((TM, dv), BF16)],
        compiler_params=_params(),
        name=name,
    )(*xs, q, kt, kt, kt_cache, v, v, v_cache, *smalls, mods, wo, g2)


def _conv3(zs_ref, w, off):
    z = zs_ref[...]
    n = z.shape[0]
    z_prev = pltpu.roll(z, 1, 0)[off:off + TM]
    z_next = pltpu.roll(z, n - 1, 0)[off:off + TM]
    return z_prev * w[0:1] + z[off:off + TM] * w[1:2] + z_next * w[2:3]


def _ffn_kernel(n_out, y_ref, hf_ref, prev_ref, next_ref, mod_ref, win_ref, cw_ref, cb_ref, wout_ref,
                *rest):
    out_refs, (lhs_ref, z_ref, act_ref) = rest[:n_out], rest[n_out:]
    has_prev, has_next = _seq_edges(pl.program_id(0))
    hb = HALO_BF16
    zeros = jnp.zeros((hb, D), BF16)
    lhs_ref[0:hb, :] = jnp.where(has_prev, prev_ref[...], zeros)
    lhs_ref[hb:hb + TM, :] = hf_ref[...]
    lhs_ref[hb + TM:, :] = jnp.where(has_next, next_ref[...], zeros)
    lhs = lhs_ref[...]
    for c in range(FFN_HIDDEN // HC):
        gs = slice(c * HC, (c + 1) * HC)
        us = slice(FFN_HIDDEN + c * HC, FFN_HIDDEN + (c + 1) * HC)
        z_ref[:, gs] = _dot(lhs, win_ref[0, :, gs])
        z_ref[:, us] = _dot(lhs, win_ref[0, :, us])
        g = _conv3(z_ref.at[:, gs], cw_ref[0, :, gs], hb) + cb_ref[0, :, gs]
        u = _conv3(z_ref.at[:, us], cw_ref[0, :, us], hb) + cb_ref[0, :, us]
        act_ref[:, gs] = ((g * _sigmoid(g)) * u).astype(BF16)
    gate2 = _mod_parts(mod_ref)[5]
    out = y_ref[...] + gate2 * _dot(act_ref[...], wout_ref[0])
    if len(out_refs) == 1:
        out_refs[0][...] = out
    else:
        @pl.when(pl.program_id(0) < NB_P)
        def _():
            out_refs[0][...] = out

        @pl.when(pl.program_id(0) >= NB_P)
        def _():
            out_refs[1][...] = out


def _ffn(y, hf, mods, layer, win, cw, cb, wout, split_out=False):
    hb = HALO_BF16
    r = TM // hb
    if split_out:
        out_shape = (jax.ShapeDtypeStruct((T_P, D), F32), jax.ShapeDtypeStruct((T_S, D), F32))
        out_specs = tuple(_split_specs(D))
    else:
        out_shape, out_specs = (jax.ShapeDtypeStruct((T, D), F32),), (_row_spec(D),)
    return pl.pallas_call(
        functools.partial(_ffn_kernel, len(out_shape)),
        out_shape=out_shape,
        grid=(NB,),
        in_specs=[_row_spec(D), _row_spec(D),
                  pl.BlockSpec((hb, D), lambda i: (jnp.maximum(i * r - 1, 0), 0)),
                  pl.BlockSpec((hb, D), lambda i: (jnp.minimum((i + 1) * r, T // hb - 1), 0)),
                  _mod_spec(layer), *[pl.BlockSpec((1,) + w.shape[1:], lambda i: (layer, 0, 0))
                                      for w in (win, cw, cb, wout)]],
        out_specs=out_specs,
        scratch_shapes=[pltpu.VMEM((TM + 2 * hb, D), BF16),
                        pltpu.VMEM((TM + 2 * hb, 2 * FFN_HIDDEN), F32),
                        pltpu.VMEM((TM, FFN_HIDDEN), BF16)],
        compiler_params=_params(),
        name="ffn",
    )(y, hf, hf, hf, mods, win, cw, cb, wout)


def _diff_in_kernel(x_ref, mod_ref, g1_ref, w_ref, qn_ref, kn_ref, cos_ref, sin_ref,
                    q_ref, kt_ref, v_ref, kst_ref, vst_ref, kn_ref_scratch):
    shift, scale = _mod_parts(mod_ref)[:2]
    h = _modulate(x_ref[...], g1_ref[...], shift, scale).astype(BF16)
    z = _dot(h, w_ref[...])
    lane = lax.broadcasted_iota(jnp.int32, (1, LANE), 1)
    left = lane < DIFF_HD
    first = (lane % 64) < 32
    cos, sin = cos_ref[...], sin_ref[...]
    v = z[:, 2 * D:]
    v_ref[...] = v.astype(BF16)
    for hh in range(DIFF_HEADS):
        cs = slice(hh * LANE, (hh + 1) * LANE)
        qn = _group_rms(z[:, hh * LANE:(hh + 1) * LANE], qn_ref[...], left)
        q_ref[:, cs] = (_rope128(qn, cos, sin, first) * DIFF_HD ** -0.5).astype(BF16)
        kn = _group_rms(z[:, D + hh * LANE:D + (hh + 1) * LANE], kn_ref[...], left)
        kt_ref[hh * LANE:(hh + 1) * LANE, :] = _rope128(kn, cos, sin, first).T.astype(BF16)
        kn_ref_scratch[:, cs] = kn

    @pl.when(pl.program_id(0) < NB_P)
    def _():
        vst_ref[...] = v
        kst_ref[...] = kn_ref_scratch[...]


def _diff_in(x, mods, layer, g1, w, qn, kn, cos_t, sin_t):
    st_spec = pl.BlockSpec((TM, D), lambda i: (jnp.minimum(i, NB_P - 1), 0))
    return pl.pallas_call(
        _diff_in_kernel,
        out_shape=(jax.ShapeDtypeStruct((T, D), BF16), jax.ShapeDtypeStruct((D, T), BF16),
                   jax.ShapeDtypeStruct((T, D), BF16), jax.ShapeDtypeStruct((T_P, D), F32),
                   jax.ShapeDtypeStruct((T_P, D), F32)),
        grid=(NB,),
        in_specs=[_row_spec(D), _mod_spec(layer), _full((1, D)), _full((D, 3 * D)),
                  _full((1, LANE)), _full((1, LANE)), _rope_spec(), _rope_spec()],
        out_specs=(_row_spec(D), pl.BlockSpec((D, TM), lambda i: (0, i)), _row_spec(D), st_spec, st_spec),
        scratch_shapes=[pltpu.VMEM((TM, D), F32)],
        compiler_params=_params(),
        name="diff_in",
    )(x, mods, g1, w, qn, kn, cos_t, sin_t)


def _sconv_kernel(x_ref, prev_ref, next_ref, mod_ref, g1_ref, win_ref, cw_ref, wout_ref, g2_ref,
                  y_ref, hf_ref, p_ref):
    has_prev, has_next = _seq_edges(pl.program_id(0))
    hb = HALO_F32
    mods = _mod_parts(mod_ref)
    x = x_ref[...]
    mod_in = lambda a: _modulate(a, g1_ref[...], mods[0], mods[1])
    zeros = jnp.zeros((hb, D), F32)
    lhs = jnp.concatenate([jnp.where(has_prev, mod_in(prev_ref[...]), zeros), mod_in(x),
                           jnp.where(has_next, mod_in(next_ref[...]), zeros)],
                          axis=0).astype(BF16)
    acc = jnp.zeros((TM, D), F32)
    for c in range(D // HC):
        cs = slice(c * HC, (c + 1) * HC)
        gb = _dot(lhs, win_ref[:, c * HC:(c + 1) * HC])
        gc = _dot(lhs, win_ref[:, D + c * HC:D + (c + 1) * HC])
        u = _dot(lhs, win_ref[:, 2 * D + c * HC:2 * D + (c + 1) * HC])
        p_ref[...] = gc * u
        mixed = gb[hb:hb + TM] * _conv3(p_ref, cw_ref[:, cs], hb)
        acc = acc + _dot(mixed.astype(BF16), wout_ref[cs, :])
    y, hf = _residual_and_ffn_in(x, acc, mods, g2_ref[...])
    y_ref[...] = y
    hf_ref[...] = hf


def _sconv(x, mods, layer, g1, win, cw, wout, g2):
    hb = HALO_F32
    r = TM // hb
    return pl.pallas_call(
        _sconv_kernel,
        out_shape=(jax.ShapeDtypeStruct((T, D), F32), jax.ShapeDtypeStruct((T, D), BF16)),
        grid=(NB,),
        in_specs=[_row_spec(D),
                  pl.BlockSpec((hb, D), lambda i: (jnp.maximum(i * r - 1, 0), 0)),
                  pl.BlockSpec((hb, D), lambda i: (jnp.minimum((i + 1) * r, T // hb - 1), 0)),
                  _mod_spec(layer), _full((1, D)), _full((D, 3 * D)), _full((3, D)),
                  _full((D, D)), _full((1, D))],
        out_specs=(_row_spec(D), _row_spec(D)),
        scratch_shapes=[pltpu.VMEM((TM + 2 * hb, HC), F32)],
        compiler_params=_params(),
        name="sconv",
    )(x, x, x, mods, g1, win, cw, wout, g2)


def _gmlp_kernel(x_ref, mod_ref, g1_ref, win_ref, vn_ref, ws_ref, bs_ref, wout_ref, g2_ref,
                 y_ref, hf_ref, gated_ref):
    mods = _mod_parts(mod_ref)
    x = x_ref[...]
    h = _modulate(x, g1_ref[...], mods[0], mods[1]).astype(BF16)
    z = _dot(h, win_ref[...])
    z = z * (0.5 * (1.0 + jnp.tanh(math.sqrt(2.0 / math.pi) * (z + 0.044715 * (z * z * z)))))
    u = z[:, :D]
    v = _rms(z[:, D:], vn_ref[...]).astype(BF16)
    bs = bs_ref[...]
    for g in range(GMLP_GROUPS):
        cs = slice(g * LANE, (g + 1) * LANE)
        w = ws_ref[g]
        for r in range(TM // GMLP_CHUNK):
            rs = slice(r * GMLP_CHUNK, (r + 1) * GMLP_CHUNK)
            mixed = _dot(w, v[rs, cs]) + bs[:, g:g + 1]
            gated_ref[rs, cs] = (u[rs, cs] * mixed).astype(BF16)
    y, hf = _residual_and_ffn_in(x, _dot(gated_ref[...], wout_ref[...]), mods, g2_ref[...])
    y_ref[...] = y
    hf_ref[...] = hf


def _gmlp(x, mods, layer, g1, win, vn, ws, bs_t, wout, g2):
    return pl.pallas_call(
        _gmlp_kernel,
        out_shape=(jax.ShapeDtypeStruct((T, D), F32), jax.ShapeDtypeStruct((T, D), BF16)),
        grid=(NB,),
        in_specs=[_row_spec(D), _mod_spec(layer), _full((1, D)), _full((D, 2 * D)), _full((1, D)),
                  _full((GMLP_GROUPS, GMLP_CHUNK, GMLP_CHUNK)), _full((GMLP_CHUNK, GMLP_GROUPS)),
                  _full((D, D)), _full((1, D))],
        out_specs=(_row_spec(D), _row_spec(D)),
        scratch_shapes=[pltpu.VMEM((TM, D), BF16)],
        compiler_params=_params(),
        name="gmlp",
    )(x, mods, g1, win, vn, ws, bs_t, wout, g2)


def _rope_tables():
    rows = DEC_SEQ // GRID_W
    row = jnp.repeat(jnp.arange(rows, dtype=F32), GRID_W)
    col = jnp.tile(jnp.arange(GRID_W, dtype=F32), rows)
    n_freq = MLA_ROPE // 4
    inv_freq = ROPE_THETA ** (-jnp.arange(n_freq, dtype=F32) / n_freq)
    ang = jnp.concatenate([row[:, None] * inv_freq, col[:, None] * inv_freq], axis=-1)
    cos, sin = jnp.cos(ang), jnp.sin(ang)
    cos = jnp.concatenate([cos, jnp.ones((TM, cos.shape[1]), F32)], axis=0)
    sin = jnp.concatenate([sin, jnp.zeros((TM, sin.shape[1]), F32)], axis=0)
    return jnp.tile(cos, (1, 4)), jnp.tile(jnp.concatenate([-sin, sin], axis=1), (1, 2))


def _pad_lanes(a, width=LANE):
    return jnp.pad(a, [(0, 0)] * (a.ndim - 1) + [(0, width - a.shape[-1])])


def _heads_first(a):
    return jnp.swapaxes(a, -3, -2).reshape(a.shape[:-3] + (D,))


def kernel(x_prompt, x_sample, cache_mla_ckv, cache_mla_krope, cache_diff_k, cache_diff_v, c, c_ctx, ada_w, ada_b, norm1_g, norm2_g, mla_w_down, mla_q_norm, mla_kv_norm, mla_w_uq, mla_w_uk, mla_w_uv, mla_qn_nope, mla_qn_rope, mla_kn_nope, mla_kn_rope, mla_w_o, diff_w_qkv, diff_qn, diff_kn, diff_lq1, diff_lk1, diff_lq2, diff_lk2, diff_head_norm, diff_w_o, sconv_w_in, sconv_w, sconv_w_out, gmlp_w_in, gmlp_v_norm, gmlp_w_s, gmlp_b_s, gmlp_w_out, ffn_w_in, ffn_conv_w, ffn_conv_b, ffn_w_out):
    xp, xs = x_prompt.reshape(T_P, D), x_sample.reshape(T_S, D)
    cond = jnp.concatenate([c_ctx[None], c, jnp.zeros((MOD_ROWS - 1 - DEC_BATCH, D), F32)], axis=0)
    mods = _adaln(cond, ada_w, ada_b).reshape(DEPTH * MOD_ROWS, 1, N_MOD * D)
    cos_t, sin_t = _rope_tables()
    row = lambda a: a.reshape(1, -1)

    ffn_w = (ffn_w_in.astype(BF16), ffn_conv_w, ffn_conv_b[:, None, :], ffn_w_out.astype(BF16))

    def ffn(y, hf, i):
        return _ffn(y, hf, mods, i, *ffn_w, split_out=(i == DEPTH - 1))

    wd = _pad_lanes(mla_w_down[0], MLA_DOWN_W).astype(BF16)
    wuq = _pad_lanes(mla_w_uq[0].reshape(MLA_Q_LORA, MLA_HEADS, MLA_NOPE + MLA_ROPE), MLA_DK)
    wuq = wuq.reshape(MLA_Q_LORA, MLA_HEADS * MLA_DK).astype(BF16)
    q, ckv, kr = _mla_in(xp, xs, mods, 0, row(norm1_g[0]), wd, row(mla_q_norm[0]), row(mla_kv_norm[0]),
                         _pad_lanes(row(mla_kn_rope[0])), wuq, row(mla_qn_nope[0]),
                         _pad_lanes(row(mla_qn_rope[0])), cos_t, sin_t)
    state_ckv = ckv[:T_P].reshape(BATCH, 1, SEQ, MLA_KV_LORA)
    state_kr = kr[:T_P, :MLA_ROPE].reshape(BATCH, 1, SEQ, MLA_ROPE)
    kv_w = (mla_w_uk[0].astype(BF16), mla_w_uv[0].astype(BF16), row(mla_kn_nope[0]))
    kt, v = _mla_kv(ckv, kr, *kv_w)
    kt_cache, v_cache = _mla_kv(cache_mla_ckv[:, 0].reshape(DEC_BATCH * PAST, MLA_KV_LORA),
                                _pad_lanes(cache_mla_krope[:, 0]).reshape(DEC_BATCH * PAST, LANE), *kv_w)
    y, hf = _attention(_mla_heads, "mla_attn", (xp, xs), q, kt, v, kt_cache, v_cache, [], mods, 0,
                       mla_w_o[0].astype(BF16), row(norm2_g[0]), MLA_HEADS * MLA_DK, MLA_HEADS * MLA_V)
    x, = ffn(y, hf, 0)

    lam_init = 0.8 - 0.6 * math.exp(-0.3 * 1)
    wq, wk, wv = jnp.split(diff_w_qkv[0], 3, axis=-1)
    perm = lambda w: _heads_first(w.reshape(D, 2, DIFF_HEADS, DIFF_HD))
    w_qkv = jnp.concatenate([perm(wq), perm(wk), wv], axis=-1).astype(BF16)
    pair = lambda g: row(jnp.tile(g, 2))
    q, kt, v, kst, vst = _diff_in(x, mods, 1, row(norm1_g[1]), w_qkv, pair(diff_qn[0]), pair(diff_kn[0]),
                                 cos_t, sin_t)
    state_dk = jnp.swapaxes(kst.reshape(BATCH, SEQ, DIFF_HEADS, 2, DIFF_HD), 2, 3)[:, None]
    state_dv = vst.reshape(BATCH, 1, SEQ, DIFF_HEADS, 2 * DIFF_HD)
    kt_cache = _heads_first(cache_diff_k[:, 0]).astype(BF16).reshape(DEC_BATCH * PAST, D).T
    v_cache = cache_diff_v[:, 0].reshape(DEC_BATCH * PAST, D).astype(BF16)
    smalls = [row(diff_lq1[0]), row(diff_lk1[0]), row(diff_lq2[0]), row(diff_lk2[0]),
              row(diff_head_norm[0])]
    y, hf = _attention(functools.partial(_diff_heads, lam_init), "diff_attn", (x,), q, kt, v, kt_cache, v_cache,
                       smalls, mods, 1, diff_w_o[0].astype(BF16), row(norm2_g[1]), D, D)
    x, = ffn(y, hf, 1)

    y, hf = _sconv(x, mods, 2, row(norm1_g[2]), sconv_w_in[0].astype(BF16), sconv_w[0],
                   sconv_w_out[0].astype(BF16), row(norm2_g[2]))
    x, = ffn(y, hf, 2)

    y, hf = _gmlp(x, mods, 3, row(norm1_g[3]), gmlp_w_in[0].astype(BF16), row(gmlp_v_norm[0]),
                  gmlp_w_s[0].astype(BF16), gmlp_b_s[0].T, gmlp_w_out[0].astype(BF16), row(norm2_g[3]))
    yp, ys = ffn(y, hf, 3)

    return (yp.reshape(BATCH, SEQ, D), ys.reshape(DEC_BATCH, DEC_SEQ, D),
            state_ckv, state_kr, state_dk, state_dv)
```

```python
import functools
import math

import jax
import jax.numpy as jnp
from jax import lax
from jax.experimental import pallas as pl
from jax.experimental.pallas import tpu as pltpu

D = 1024
BATCH, SEQ = 16, 256
DEC_BATCH, DEC_SEQ = 8, 2048
PAST = 256
DEPTH = 4
GRID_W = 64
EPS = 1e-6
ROPE_THETA = 10000.0
N_MOD = 6
MLA_HEADS, MLA_NOPE, MLA_ROPE, MLA_V = 8, 128, 64, 128
MLA_Q_LORA, MLA_KV_LORA = 768, 256
DIFF_HEADS, DIFF_HD = 8, 64
GMLP_CHUNK, GMLP_GROUPS = 128, 8
FFN_HIDDEN = 2816

T_P = BATCH * SEQ
T_S = DEC_BATCH * DEC_SEQ
T = T_P + T_S
TM = 256
NB_P = T_P // TM
NB = T // TM
BPS = DEC_SEQ // TM
KV_LEN = PAST + DEC_SEQ
MOD_ROWS = 16
LANE = 128
MLA_DK = 256
HC = 256
HALO_BF16 = 16
HALO_F32 = 8
VMEM_LIMIT = 56 * 1024 * 1024

F32 = jnp.float32
BF16 = jnp.bfloat16


def _params(n_axes=1, vmem=VMEM_LIMIT):
    return pltpu.CompilerParams(dimension_semantics=("arbitrary",) * n_axes,
                                vmem_limit_bytes=vmem)


def _dot(a, b):
    return jnp.dot(a, b, preferred_element_type=F32)


def _rms(x, g):
    ms = jnp.mean(x * x, axis=-1, keepdims=True)
    return (x * lax.rsqrt(ms + EPS)) * g


def _modulate(x, g, shift, scale):
    return _rms(x, g) * (1.0 + scale) + shift


def _sigmoid(x):
    return 1.0 / (1.0 + jnp.exp(-x))


def _mod_parts(mod_ref):
    m = mod_ref[0]
    return [m[:, k * D:(k + 1) * D] for k in range(N_MOD)]


def _full(shape):
    return pl.BlockSpec(shape, lambda *_: (0,) * len(shape))


def _row_spec(width, rows=TM):
    return pl.BlockSpec((rows, width), lambda i: (i, 0))


def _split_specs(width, tm=TM):
    nbp = T_P // tm
    return [pl.BlockSpec((tm, width), lambda i: (jnp.minimum(i, nbp - 1), 0)),
            pl.BlockSpec((tm, width), lambda i: (jnp.maximum(i - nbp, 0), 0))]


def _split_rows(p_ref, s_ref):
    return jnp.where(pl.program_id(0) < NB_P, p_ref[...], s_ref[...])


def _mod_spec(layer, tm=TM):
    nbp, bps = T_P // tm, DEC_SEQ // tm

    def imap(i):
        j = jnp.maximum(i - nbp, 0)
        return (layer * MOD_ROWS + jnp.where(i < nbp, 0, 1 + j // bps), 0, 0)
    return pl.BlockSpec((1, 1, N_MOD * D), imap)


def _rope_spec():
    def imap(i):
        j = jnp.maximum(i - NB_P, 0)
        return (jnp.where(i < NB_P, BPS, j % BPS), 0)
    return pl.BlockSpec((TM, LANE), imap)


def _seq_edges(i, tm=TM):
    row0 = i * tm
    seq_len = jnp.where(row0 < T_P, SEQ, DEC_SEQ)
    return row0 % seq_len != 0, (row0 + tm) % seq_len != 0


def _group_rms(xb, gain, group_ones):
    sq = xb * xb
    hi = sq.astype(BF16)
    lo = (sq - hi.astype(F32)).astype(BF16)
    ms = (_dot(hi, group_ones) + _dot(lo, group_ones)) * (1.0 / 64.0)
    return (xb * lax.rsqrt(ms + EPS)) * gain


def _rope128(xb, cos, sin, first):
    swapped = jnp.where(first, pltpu.roll(xb, LANE - 32, 1), pltpu.roll(xb, 32, 1))
    return xb * cos + swapped * sin


ADA_TN = 1536


def _adaln_kernel(cond_ref, w_ref, b_ref, o_ref):
    c = cond_ref[...]
    a = (c * _sigmoid(c)).astype(BF16)
    o_ref[0] = _dot(a, w_ref[0].astype(BF16)) + b_ref[0]


def _adaln(cond, ada_w, ada_b):
    n = N_MOD * D
    return pl.pallas_call(
        _adaln_kernel,
        out_shape=jax.ShapeDtypeStruct((DEPTH, MOD_ROWS, n), F32),
        grid=(DEPTH, n // ADA_TN),
        in_specs=[pl.BlockSpec((MOD_ROWS, D), lambda l, j: (0, 0)),
                  pl.BlockSpec((1, D, ADA_TN), lambda l, j: (l, 0, j)),
                  pl.BlockSpec((1, 1, ADA_TN), lambda l, j: (l, 0, j))],
        out_specs=pl.BlockSpec((1, MOD_ROWS, ADA_TN), lambda l, j: (l, 0, j)),
        compiler_params=_params(2),
        name="adaln",
    )(cond, ada_w, ada_b.reshape(DEPTH, 1, n))


MLA_DOWN_W = MLA_Q_LORA + MLA_KV_LORA + LANE


def _mla_in_kernel(xp_ref, xs_ref, mod_ref, g1_ref, wd_ref, qn_ref, kvn_ref, knr_ref, wuq_ref,
                   qnn_ref, qnr_ref, cos_ref, sin_ref, q_ref, ckv_ref, kr_ref):
    shift, scale = _mod_parts(mod_ref)[:2]
    h = _modulate(_split_rows(xp_ref, xs_ref), g1_ref[...], shift, scale).astype(BF16)
    d = _dot(h, wd_ref[...])
    cq = _rms(d[:, :MLA_Q_LORA], qn_ref[...]).astype(BF16)
    ckv_ref[...] = _rms(d[:, MLA_Q_LORA:MLA_Q_LORA + MLA_KV_LORA], kvn_ref[...])

    lane = lax.broadcasted_iota(jnp.int32, (1, LANE), 1)
    first = (lane % 64) < 32
    cos, sin = cos_ref[...], sin_ref[...]

    def rope_part(xb, gain):
        ms = jnp.sum(xb * xb, axis=-1, keepdims=True) * (1.0 / MLA_ROPE)
        return _rope128((xb * lax.rsqrt(ms + EPS)) * gain, cos, sin, first)

    kr_ref[...] = rope_part(d[:, MLA_Q_LORA + MLA_KV_LORA:], knr_ref[...])

    q = _dot(cq, wuq_ref[...])
    for hh in range(MLA_HEADS):
        c0 = hh * MLA_DK
        q_ref[:, c0:c0 + LANE] = _rms(q[:, c0:c0 + LANE], qnn_ref[...]).astype(BF16)
        q_ref[:, c0 + LANE:c0 + MLA_DK] = rope_part(q[:, c0 + LANE:c0 + MLA_DK],
                                                    qnr_ref[...]).astype(BF16)


def _mla_in(xp, xs, mods, layer, g1, wd, qn, kvn, knr, wuq, qnn, qnr, cos_t, sin_t):
    return pl.pallas_call(
        _mla_in_kernel,
        out_shape=(jax.ShapeDtypeStruct((T, MLA_HEADS * MLA_DK), BF16),
                   jax.ShapeDtypeStruct((T, MLA_KV_LORA), F32),
                   jax.ShapeDtypeStruct((T, LANE), F32)),
        grid=(NB,),
        in_specs=[*_split_specs(D), _mod_spec(layer), _full((1, D)), _full((D, MLA_DOWN_W)),
                  _full((1, MLA_Q_LORA)), _full((1, MLA_KV_LORA)), _full((1, LANE)),
                  _full((MLA_Q_LORA, MLA_HEADS * MLA_DK)), _full((1, LANE)), _full((1, LANE)),
                  _rope_spec(), _rope_spec()],
        out_specs=(_row_spec(MLA_HEADS * MLA_DK), _row_spec(MLA_KV_LORA), _row_spec(LANE)),
        compiler_params=_params(),
        name="mla_in",
    )(xp, xs, mods, g1, wd, qn, kvn, knr, wuq, qnn, qnr, cos_t, sin_t)


def _mla_kv_kernel(ckv_ref, kr_ref, wuk_ref, wuv_ref, knn_ref, kt_ref, v_ref):
    c = ckv_ref[...].astype(BF16)
    kn = _dot(c, wuk_ref[...])
    v_ref[...] = _dot(c, wuv_ref[...]).astype(BF16)
    kr_t = kr_ref[...].T.astype(BF16)
    for hh in range(MLA_HEADS):
        r0 = hh * MLA_DK
        kh = _rms(kn[:, hh * LANE:(hh + 1) * LANE], knn_ref[...])
        kt_ref[r0:r0 + LANE, :] = kh.T.astype(BF16)
        kt_ref[r0 + LANE:r0 + MLA_DK, :] = kr_t


def _mla_kv(ckv, kr, wuk, wuv, knn):
    n = ckv.shape[0]
    return pl.pallas_call(
        _mla_kv_kernel,
        out_shape=(jax.ShapeDtypeStruct((MLA_HEADS * MLA_DK, n), BF16),
                   jax.ShapeDtypeStruct((n, MLA_HEADS * MLA_V), BF16)),
        grid=(n // TM,),
        in_specs=[_row_spec(MLA_KV_LORA), _row_spec(LANE),
                  _full((MLA_KV_LORA, MLA_HEADS * MLA_NOPE)), _full((MLA_KV_LORA, MLA_HEADS * MLA_V)),
                  _full((1, LANE))],
        out_specs=(pl.BlockSpec((MLA_HEADS * MLA_DK, TM), lambda i: (0, i)), _row_spec(MLA_HEADS * MLA_V)),
        compiler_params=_params(),
        name="mla_kv",
    )(ckv, kr, wuk, wuv, knn)


LOG2E = math.log2(math.e)


def _scores(q, kts, rows):
    parts = [_dot(q, kt_ref[rows, :]) for kt_ref in kts]
    return parts[0] if len(parts) == 1 else jnp.concatenate(parts, axis=1)


def _softmax_pv(s2, vs, cols):
    p = jnp.exp2(s2 - jnp.max(s2, axis=-1, keepdims=True)).astype(BF16)
    o, k0 = None, 0
    for v_ref in vs:
        v = v_ref[:, cols]
        part = _dot(p[:, k0:k0 + v.shape[0]], jnp.concatenate([v, jnp.ones_like(v)], axis=1))
        o = part if o is None else o + part
        k0 += v.shape[0]
    return o[:, :LANE] * (1.0 / o[:, LANE:LANE + 1])


def _mla_heads(q_ref, kts, vs, o_ref):
    scale2 = (MLA_NOPE + MLA_ROPE) ** -0.5 * LOG2E
    for hh in range(MLA_HEADS):
        ks = slice(hh * MLA_DK, (hh + 1) * MLA_DK)
        cs = slice(hh * MLA_V, (hh + 1) * MLA_V)
        o_ref[:, cs] = _softmax_pv(_scores(q_ref[:, ks], kts, ks) * scale2, vs, cs).astype(BF16)


def _diff_heads(lam_init, lq1_ref, lk1_ref, lq2_ref, lk2_ref, hn_ref, q_ref, kts, vs, o_ref):
    lam = (jnp.exp(jnp.sum(lq1_ref[...] * lk1_ref[...], axis=-1, keepdims=True))
           - jnp.exp(jnp.sum(lq2_ref[...] * lk2_ref[...], axis=-1, keepdims=True)) + lam_init)
    lane = lax.broadcasted_iota(jnp.int32, (1, LANE), 1)
    left = lane < DIFF_HD
    zero = jnp.zeros((), BF16)
    for hh in range(DIFF_HEADS):
        cs = slice(hh * LANE, (hh + 1) * LANE)
        qb = q_ref[:, cs]
        o0 = _softmax_pv(_scores(jnp.where(left, qb, zero), kts, cs) * LOG2E, vs, cs)
        o1 = _softmax_pv(_scores(jnp.where(left, zero, qb), kts, cs) * LOG2E, vs, cs)
        o = o0 - lam * o1
        o_ref[:, cs] = (_rms(o, hn_ref[...]) * (1.0 - lam_init)).astype(BF16)


def _residual_and_ffn_in(x, mix, mods, g2):
    _, _, gate1, shift2, scale2, _ = mods
    y = x + gate1 * mix
    return y, _modulate(y, g2, shift2, scale2).astype(BF16)


def _attn_kernel(heads, n_x, n_small, *refs):
    x_refs, refs = refs[:n_x], refs[n_x:]
    q_ref, ktp_ref, ktn_ref, ktc_ref, vp_ref, vn_ref, vc_ref = refs[:7]
    smalls, (mod_ref, wo_ref, g2_ref, y_ref, hf_ref, o_ref) = refs[7:7 + n_small], refs[7 + n_small:]
    is_prompt = pl.program_id(0) < NB_P

    @pl.when(is_prompt)
    def _():
        heads(*smalls, q_ref, [ktp_ref], [vp_ref], o_ref)

    @pl.when(jnp.logical_not(is_prompt))
    def _():
        heads(*smalls, q_ref, [ktc_ref, ktn_ref], [vc_ref, vn_ref], o_ref)

    x = x_refs[0][...] if n_x == 1 else _split_rows(*x_refs)
    y, hf = _residual_and_ffn_in(x, _dot(o_ref[...], wo_ref[...]), _mod_parts(mod_ref), g2_ref[...])
    y_ref[...] = y
    hf_ref[...] = hf


def _attention(heads, name, xs, q, kt, v, kt_cache, v_cache, smalls, mods, layer, wo, g2, dk, dv):
    assert SEQ == TM and T_P % DEC_SEQ == 0
    pro = lambda i: jnp.minimum(i, NB_P - 1)
    lat = lambda i: jnp.maximum(i - NB_P, 0) // BPS
    lat_new = lambda i: T_P // DEC_SEQ + lat(i)
    x_specs = [_row_spec(D)] if len(xs) == 1 else _split_specs(D)
    return pl.pallas_call(
        functools.partial(_attn_kernel, heads, len(xs), len(smalls)),
        out_shape=(jax.ShapeDtypeStruct((T, D), F32), jax.ShapeDtypeStruct((T, D), BF16)),
        grid=(NB,),
        in_specs=[*x_specs, _row_spec(dk),
                  pl.BlockSpec((dk, SEQ), lambda i: (0, pro(i))),
                  pl.BlockSpec((dk, DEC_SEQ), lambda i: (0, lat_new(i))),
                  pl.BlockSpec((dk, PAST), lambda i: (0, lat(i))),
                  pl.BlockSpec((SEQ, dv), lambda i: (pro(i), 0)),
                  pl.BlockSpec((DEC_SEQ, dv), lambda i: (lat_new(i), 0)),
                  pl.BlockSpec((PAST, dv), lambda i: (lat(i), 0)),
                  *[_full(s.shape) for s in smalls], _mod_spec(layer), _full((dv, D)), _full((1, D))],
        out_specs=(_row_spec(D), _row_spec(D)),
        scratch_shapes=[pltpu.VMEM((TM, dv), BF16)],
        compiler_params=_params(),
        name=name,
    )(*xs, q, kt, kt, kt_cache, v, v, v_cache, *smalls, mods, wo, g2)


def _zero_row(a, r, cond):
    r0 = r // HALO_F32 * HALO_F32
    tile = a[r0:r0 + HALO_F32]
    row = lax.broadcasted_iota(jnp.int32, (HALO_F32, 1), 0)
    tile = jnp.where(row == r - r0, jnp.where(cond, 0.0, tile), tile)
    return jnp.concatenate([a[:r0], tile, a[r0 + HALO_F32:]], axis=0)


def _conv3(zs_ref, w, off, tm=TM, is_prompt=False):
    z = zs_ref[...]
    n = z.shape[0]
    z_prev = pltpu.roll(z, 1, 0)[off:off + tm]
    z_next = pltpu.roll(z, n - 1, 0)[off:off + tm]
    for start in range(SEQ, tm, SEQ):
        z_prev = _zero_row(z_prev, start, is_prompt)
        z_next = _zero_row(z_next, start - 1, is_prompt)
    return z_prev * w[0:1] + z[off:off + tm] * w[1:2] + z_next * w[2:3]


FFN_TM = 512


def _ffn_kernel(n_out, y_ref, hf_ref, prev_ref, next_ref, mod_ref, win_ref, cw_ref, cb_ref, wout_ref,
                *rest):
    out_refs, (lhs_ref, z_ref, act_ref) = rest[:n_out], rest[n_out:]
    tm, hb = FFN_TM, HALO_BF16
    is_prompt = pl.program_id(0) < T_P // tm
    has_prev, has_next = _seq_edges(pl.program_id(0), tm)
    zeros = jnp.zeros((hb, D), BF16)
    lhs_ref[0:hb, :] = jnp.where(has_prev, prev_ref[...], zeros)
    lhs_ref[hb:hb + tm, :] = hf_ref[...]
    lhs_ref[hb + tm:, :] = jnp.where(has_next, next_ref[...], zeros)
    lhs = lhs_ref[...]
    for c in range(FFN_HIDDEN // HC):
        gs = slice(c * HC, (c + 1) * HC)
        us = slice(FFN_HIDDEN + c * HC, FFN_HIDDEN + (c + 1) * HC)
        z_ref[:, gs] = _dot(lhs, win_ref[0, :, gs])
        z_ref[:, us] = _dot(lhs, win_ref[0, :, us])
        g = _conv3(z_ref.at[:, gs], cw_ref[0, :, gs], hb, tm, is_prompt) + cb_ref[0, :, gs]
        u = _conv3(z_ref.at[:, us], cw_ref[0, :, us], hb, tm, is_prompt) + cb_ref[0, :, us]
        act_ref[:, gs] = ((g * _sigmoid(g)) * u).astype(BF16)
    gate2 = _mod_parts(mod_ref)[5]
    out = y_ref[...] + gate2 * _dot(act_ref[...], wout_ref[0])
    if len(out_refs) == 1:
        out_refs[0][...] = out
    else:
        @pl.when(is_prompt)
        def _():
            out_refs[0][...] = out

        @pl.when(jnp.logical_not(is_prompt))
        def _():
            out_refs[1][...] = out


def _ffn(y, hf, mods, layer, win, cw, cb, wout, split_out=False):
    tm, hb = FFN_TM, HALO_BF16
    r = tm // hb
    if split_out:
        out_shape = (jax.ShapeDtypeStruct((T_P, D), F32), jax.ShapeDtypeStruct((T_S, D), F32))
        out_specs = tuple(_split_specs(D, tm))
    else:
        out_shape, out_specs = (jax.ShapeDtypeStruct((T, D), F32),), (_row_spec(D, tm),)
    w_specs = [pl.BlockSpec((1,) + w.shape[1:], lambda i: (layer, 0, 0), pipeline_mode=pl.Buffered(1))
               for w in (win, cw, cb, wout)]
    return pl.pallas_call(
        functools.partial(_ffn_kernel, len(out_shape)),
        out_shape=out_shape,
        grid=(T // tm,),
        in_specs=[_row_spec(D, tm), _row_spec(D, tm),
                  pl.BlockSpec((hb, D), lambda i: (jnp.maximum(i * r - 1, 0), 0)),
                  pl.BlockSpec((hb, D), lambda i: (jnp.minimum((i + 1) * r, T // hb - 1), 0)),
                  _mod_spec(layer, tm), *w_specs],
        out_specs=out_specs,
        scratch_shapes=[pltpu.VMEM((tm + 2 * hb, D), BF16),
                        pltpu.VMEM((tm + 2 * hb, 2 * FFN_HIDDEN), F32),
                        pltpu.VMEM((tm, FFN_HIDDEN), BF16)],
        compiler_params=_params(),
        name="ffn",
    )(y, hf, hf, hf, mods, win, cw, cb, wout)


def _diff_in_kernel(x_ref, mod_ref, g1_ref, w_ref, qn_ref, kn_ref, cos_ref, sin_ref,
                    q_ref, kt_ref, v_ref, kst_ref, vst_ref, kn_ref_scratch):
    shift, scale = _mod_parts(mod_ref)[:2]
    h = _modulate(x_ref[...], g1_ref[...], shift, scale).astype(BF16)
    z = _dot(h, w_ref[...])
    lane = lax.broadcasted_iota(jnp.int32, (1, LANE), 1)
    first = (lane % 64) < 32
    same_half = (lax.broadcasted_iota(jnp.int32, (LANE, LANE), 0) < DIFF_HD) == (
        lax.broadcasted_iota(jnp.int32, (LANE, LANE), 1) < DIFF_HD)
    group_ones = jnp.where(same_half, 1.0, 0.0).astype(BF16)
    cos, sin = cos_ref[...], sin_ref[...]
    v = z[:, 2 * D:]
    v_ref[...] = v.astype(BF16)
    for hh in range(DIFF_HEADS):
        cs = slice(hh * LANE, (hh + 1) * LANE)
        qn = _group_rms(z[:, hh * LANE:(hh + 1) * LANE], qn_ref[...], group_ones)
        q_ref[:, cs] = (_rope128(qn, cos, sin, first) * DIFF_HD ** -0.5).astype(BF16)
        kn = _group_rms(z[:, D + hh * LANE:D + (hh + 1) * LANE], kn_ref[...], group_ones)
        kt_ref[hh * LANE:(hh + 1) * LANE, :] = _rope128(kn, cos, sin, first).T.astype(BF16)
        kn_ref_scratch[:, cs] = kn

    @pl.when(pl.program_id(0) < NB_P)
    def _():
        vst_ref[...] = v
        kst_ref[...] = kn_ref_scratch[...]


def _diff_in(x, mods, layer, g1, w, qn, kn, cos_t, sin_t):
    st_spec = pl.BlockSpec((TM, D), lambda i: (jnp.minimum(i, NB_P - 1), 0))
    return pl.pallas_call(
        _diff_in_kernel,
        out_shape=(jax.ShapeDtypeStruct((T, D), BF16), jax.ShapeDtypeStruct((D, T), BF16),
                   jax.ShapeDtypeStruct((T, D), BF16), jax.ShapeDtypeStruct((T_P, D), F32),
                   jax.ShapeDtypeStruct((T_P, D), F32)),
        grid=(NB,),
        in_specs=[_row_spec(D), _mod_spec(layer), _full((1, D)), _full((D, 3 * D)),
                  _full((1, LANE)), _full((1, LANE)), _rope_spec(), _rope_spec()],
        out_specs=(_row_spec(D), pl.BlockSpec((D, TM), lambda i: (0, i)), _row_spec(D), st_spec, st_spec),
        scratch_shapes=[pltpu.VMEM((TM, D), F32)],
        compiler_params=_params(),
        name="diff_in",
    )(x, mods, g1, w, qn, kn, cos_t, sin_t)


def _sconv_kernel(x_ref, prev_ref, next_ref, mod_ref, g1_ref, win_ref, cw_ref, wout_ref, g2_ref,
                  y_ref, hf_ref, p_ref):
    has_prev, has_next = _seq_edges(pl.program_id(0))
    hb = HALO_F32
    mods = _mod_parts(mod_ref)
    x = x_ref[...]
    mod_in = lambda a: _modulate(a, g1_ref[...], mods[0], mods[1])
    zeros = jnp.zeros((hb, D), F32)
    lhs = jnp.concatenate([jnp.where(has_prev, mod_in(prev_ref[...]), zeros), mod_in(x),
                           jnp.where(has_next, mod_in(next_ref[...]), zeros)],
                          axis=0).astype(BF16)
    acc = jnp.zeros((TM, D), F32)
    for c in range(D // HC):
        cs = slice(c * HC, (c + 1) * HC)
        gb = _dot(lhs, win_ref[:, c * HC:(c + 1) * HC])
        gc = _dot(lhs, win_ref[:, D + c * HC:D + (c + 1) * HC])
        u = _dot(lhs, win_ref[:, 2 * D + c * HC:2 * D + (c + 1) * HC])
        p_ref[...] = gc * u
        mixed = gb[hb:hb + TM] * _conv3(p_ref, cw_ref[:, cs], hb)
        acc = acc + _dot(mixed.astype(BF16), wout_ref[cs, :])
    y, hf = _residual_and_ffn_in(x, acc, mods, g2_ref[...])
    y_ref[...] = y
    hf_ref[...] = hf


def _sconv(x, mods, layer, g1, win, cw, wout, g2):
    hb = HALO_F32
    r = TM // hb
    return pl.pallas_call(
        _sconv_kernel,
        out_shape=(jax.ShapeDtypeStruct((T, D), F32), jax.ShapeDtypeStruct((T, D), BF16)),
        grid=(NB,),
        in_specs=[_row_spec(D),
                  pl.BlockSpec((hb, D), lambda i: (jnp.maximum(i * r - 1, 0), 0)),
                  pl.BlockSpec((hb, D), lambda i: (jnp.minimum((i + 1) * r, T // hb - 1), 0)),
                  _mod_spec(layer), _full((1, D)), _full((D, 3 * D)), _full((3, D)),
                  _full((D, D)), _full((1, D))],
        out_specs=(_row_spec(D), _row_spec(D)),
        scratch_shapes=[pltpu.VMEM((TM + 2 * hb, HC), F32)],
        compiler_params=_params(),
        name="sconv",
    )(x, x, x, mods, g1, win, cw, wout, g2)


def _gmlp_kernel(x_ref, mod_ref, g1_ref, win_ref, vn_ref, ws_ref, bs_ref, wout_ref, g2_ref,
                 y_ref, hf_ref, gated_ref):
    mods = _mod_parts(mod_ref)
    x = x_ref[...]
    h = _modulate(x, g1_ref[...], mods[0], mods[1]).astype(BF16)
    z = _dot(h, win_ref[...])
    z = z * (0.5 * (1.0 + jnp.tanh(math.sqrt(2.0 / math.pi) * (z + 0.044715 * (z * z * z)))))
    u = z[:, :D]
    v = _rms(z[:, D:], vn_ref[...]).astype(BF16)
    bs = bs_ref[...]
    for g in range(GMLP_GROUPS):
        cs = slice(g * LANE, (g + 1) * LANE)
        w = ws_ref[g]
        for r in range(TM // GMLP_CHUNK):
            rs = slice(r * GMLP_CHUNK, (r + 1) * GMLP_CHUNK)
            mixed = _dot(w, v[rs, cs]) + bs[:, g:g + 1]
            gated_ref[rs, cs] = (u[rs, cs] * mixed).astype(BF16)
    y, hf = _residual_and_ffn_in(x, _dot(gated_ref[...], wout_ref[...]), mods, g2_ref[...])
    y_ref[...] = y
    hf_ref[...] = hf


def _gmlp(x, mods, layer, g1, win, vn, ws, bs_t, wout, g2):
    return pl.pallas_call(
        _gmlp_kernel,
        out_shape=(jax.ShapeDtypeStruct((T, D), F32), jax.ShapeDtypeStruct((T, D), BF16)),
        grid=(NB,),
        in_specs=[_row_spec(D), _mod_spec(layer), _full((1, D)), _full((D, 2 * D)), _full((1, D)),
                  _full((GMLP_GROUPS, GMLP_CHUNK, GMLP_CHUNK)), _full((GMLP_CHUNK, GMLP_GROUPS)),
                  _full((D, D)), _full((1, D))],
        out_specs=(_row_spec(D), _row_spec(D)),
        scratch_shapes=[pltpu.VMEM((TM, D), BF16)],
        compiler_params=_params(),
        name="gmlp",
    )(x, mods, g1, win, vn, ws, bs_t, wout, g2)


def _rope_tables():
    rows = DEC_SEQ // GRID_W
    row = jnp.repeat(jnp.arange(rows, dtype=F32), GRID_W)
    col = jnp.tile(jnp.arange(GRID_W, dtype=F32), rows)
    n_freq = MLA_ROPE // 4
    inv_freq = ROPE_THETA ** (-jnp.arange(n_freq, dtype=F32) / n_freq)
    ang = jnp.concatenate([row[:, None] * inv_freq, col[:, None] * inv_freq], axis=-1)
    cos, sin = jnp.cos(ang), jnp.sin(ang)
    cos = jnp.concatenate([cos, jnp.ones((TM, cos.shape[1]), F32)], axis=0)
    sin = jnp.concatenate([sin, jnp.zeros((TM, sin.shape[1]), F32)], axis=0)
    return jnp.tile(cos, (1, 4)), jnp.tile(jnp.concatenate([-sin, sin], axis=1), (1, 2))


def _pad_lanes(a, width=LANE):
    return jnp.pad(a, [(0, 0)] * (a.ndim - 1) + [(0, width - a.shape[-1])])


def _heads_first(a):
    return jnp.swapaxes(a, -3, -2).reshape(a.shape[:-3] + (D,))


def kernel(x_prompt, x_sample, cache_mla_ckv, cache_mla_krope, cache_diff_k, cache_diff_v, c, c_ctx, ada_w, ada_b, norm1_g, norm2_g, mla_w_down, mla_q_norm, mla_kv_norm, mla_w_uq, mla_w_uk, mla_w_uv, mla_qn_nope, mla_qn_rope, mla_kn_nope, mla_kn_rope, mla_w_o, diff_w_qkv, diff_qn, diff_kn, diff_lq1, diff_lk1, diff_lq2, diff_lk2, diff_head_norm, diff_w_o, sconv_w_in, sconv_w, sconv_w_out, gmlp_w_in, gmlp_v_norm, gmlp_w_s, gmlp_b_s, gmlp_w_out, ffn_w_in, ffn_conv_w, ffn_conv_b, ffn_w_out):
    xp, xs = x_prompt.reshape(T_P, D), x_sample.reshape(T_S, D)
    cond = jnp.concatenate([c_ctx[None], c, jnp.zeros((MOD_ROWS - 1 - DEC_BATCH, D), F32)], axis=0)
    mods = _adaln(cond, ada_w, ada_b).reshape(DEPTH * MOD_ROWS, 1, N_MOD * D)
    cos_t, sin_t = _rope_tables()
    row = lambda a: a.reshape(1, -1)

    ffn_w = (ffn_w_in.astype(BF16), ffn_conv_w, ffn_conv_b[:, None, :], ffn_w_out.astype(BF16))

    def ffn(y, hf, i):
        return _ffn(y, hf, mods, i, *ffn_w, split_out=(i == DEPTH - 1))

    wd = _pad_lanes(mla_w_down[0], MLA_DOWN_W).astype(BF16)
    wuq = _pad_lanes(mla_w_uq[0].reshape(MLA_Q_LORA, MLA_HEADS, MLA_NOPE + MLA_ROPE), MLA_DK)
    wuq = wuq.reshape(MLA_Q_LORA, MLA_HEADS * MLA_DK).astype(BF16)
    q, ckv, kr = _mla_in(xp, xs, mods, 0, row(norm1_g[0]), wd, row(mla_q_norm[0]), row(mla_kv_norm[0]),
                         _pad_lanes(row(mla_kn_rope[0])), wuq, row(mla_qn_nope[0]),
                         _pad_lanes(row(mla_qn_rope[0])), cos_t, sin_t)
    state_ckv = ckv[:T_P].reshape(BATCH, 1, SEQ, MLA_KV_LORA)
    state_kr = kr[:T_P, :MLA_ROPE].reshape(BATCH, 1, SEQ, MLA_ROPE)
    kv_w = (mla_w_uk[0].astype(BF16), mla_w_uv[0].astype(BF16), row(mla_kn_nope[0]))
    kt, v = _mla_kv(ckv, kr, *kv_w)
    kt_cache, v_cache = _mla_kv(cache_mla_ckv[:, 0].reshape(DEC_BATCH * PAST, MLA_KV_LORA),
                                _pad_lanes(cache_mla_krope[:, 0]).reshape(DEC_BATCH * PAST, LANE), *kv_w)
    y, hf = _attention(_mla_heads, "mla_attn", (xp, xs), q, kt, v, kt_cache, v_cache, [], mods, 0,
                       mla_w_o[0].astype(BF16), row(norm2_g[0]), MLA_HEADS * MLA_DK, MLA_HEADS * MLA_V)
    x, = ffn(y, hf, 0)

    lam_init = 0.8 - 0.6 * math.exp(-0.3 * 1)
    wq, wk, wv = jnp.split(diff_w_qkv[0], 3, axis=-1)
    perm = lambda w: _heads_first(w.reshape(D, 2, DIFF_HEADS, DIFF_HD))
    w_qkv = jnp.concatenate([perm(wq), perm(wk), wv], axis=-1).astype(BF16)
    pair = lambda g: row(jnp.tile(g, 2))
    q, kt, v, kst, vst = _diff_in(x, mods, 1, row(norm1_g[1]), w_qkv, pair(diff_qn[0]), pair(diff_kn[0]),
                                 cos_t, sin_t)
    state_dk = jnp.swapaxes(kst.reshape(BATCH, SEQ, DIFF_HEADS, 2, DIFF_HD), 2, 3)[:, None]
    state_dv = vst.reshape(BATCH, 1, SEQ, DIFF_HEADS, 2 * DIFF_HD)
    kt_cache = _heads_first(cache_diff_k[:, 0]).astype(BF16).reshape(DEC_BATCH * PAST, D).T
    v_cache = cache_diff_v[:, 0].reshape(DEC_BATCH * PAST, D).astype(BF16)
    smalls = [row(diff_lq1[0]), row(diff_lk1[0]), row(diff_lq2[0]), row(diff_lk2[0]),
              row(diff_head_norm[0])]
    y, hf = _attention(functools.partial(_diff_heads, lam_init), "diff_attn", (x,), q, kt, v, kt_cache, v_cache,
                       smalls, mods, 1, diff_w_o[0].astype(BF16), row(norm2_g[1]), D, D)
    x, = ffn(y, hf, 1)

    y, hf = _sconv(x, mods, 2, row(norm1_g[2]), sconv_w_in[0].astype(BF16), sconv_w[0],
                   sconv_w_out[0].astype(BF16), row(norm2_g[2]))
    x, = ffn(y, hf, 2)

    y, hf = _gmlp(x, mods, 3, row(norm1_g[3]), gmlp_w_in[0].astype(BF16), row(gmlp_v_norm[0]),
                  gmlp_w_s[0].astype(BF16), gmlp_b_s[0].T, gmlp_w_out[0].astype(BF16), row(norm2_g[3]))
    yp, ys = ffn(y, hf, 3)

    return (yp.reshape(BATCH, SEQ, D), ys.reshape(DEC_BATCH, DEC_SEQ, D),
            state_ckv, state_kr, state_dk, state_dv)
```

```python
import functools
import math

import jax
import jax.numpy as jnp
from jax import lax
from jax.experimental import pallas as pl
from jax.experimental.pallas import tpu as pltpu

D = 1024
BATCH, SEQ = 16, 256
DEC_BATCH, DEC_SEQ = 8, 2048
PAST = 256
DEPTH = 4
GRID_W = 64
EPS = 1e-6
ROPE_THETA = 10000.0
N_MOD = 6
MLA_HEADS, MLA_NOPE, MLA_ROPE, MLA_V = 8, 128, 64, 128
MLA_Q_LORA, MLA_KV_LORA = 768, 256
DIFF_HEADS, DIFF_HD = 8, 64
GMLP_CHUNK, GMLP_GROUPS = 128, 8
FFN_HIDDEN = 2816

T_P = BATCH * SEQ
T_S = DEC_BATCH * DEC_SEQ
T = T_P + T_S
TM = 256
NB_P = T_P // TM
NB = T // TM
BPS = DEC_SEQ // TM
KV_LEN = PAST + DEC_SEQ
MOD_ROWS = 16
LANE = 128
MLA_DK = 256
HC = 256
HALO_BF16 = 16
HALO_F32 = 8
VMEM_LIMIT = 56 * 1024 * 1024
LOG2E = math.log2(math.e)
MLA_Q_SCALE = (MLA_NOPE + MLA_ROPE) ** -0.5 * LOG2E
DIFF_Q_SCALE = DIFF_HD ** -0.5 * LOG2E

F32 = jnp.float32
BF16 = jnp.bfloat16


def _params(n_axes=1, vmem=VMEM_LIMIT):
    return pltpu.CompilerParams(dimension_semantics=("arbitrary",) * n_axes,
                                vmem_limit_bytes=vmem)


def _dot(a, b):
    return jnp.dot(a, b, preferred_element_type=F32)


def _rms(x, g):
    ms = jnp.mean(x * x, axis=-1, keepdims=True)
    return (x * lax.rsqrt(ms + EPS)) * g


def _modulate(x, g, shift, scale):
    return _rms(x, g) * (1.0 + scale) + shift


def _sigmoid(x):
    return 1.0 / (1.0 + jnp.exp(-x))


def _mod_parts(mod_ref):
    m = mod_ref[0]
    return [m[:, k * D:(k + 1) * D] for k in range(N_MOD)]


def _full(shape):
    return pl.BlockSpec(shape, lambda *_: (0,) * len(shape))


def _row_spec(width, rows=TM):
    return pl.BlockSpec((rows, width), lambda i: (i, 0))


def _split_specs(width, tm=TM):
    nbp = T_P // tm
    return [pl.BlockSpec((tm, width), lambda i: (jnp.minimum(i, nbp - 1), 0)),
            pl.BlockSpec((tm, width), lambda i: (jnp.maximum(i - nbp, 0), 0))]


def _split_rows(p_ref, s_ref):
    return jnp.where(pl.program_id(0) < NB_P, p_ref[...], s_ref[...])


def _mod_spec(layer, tm=TM):
    nbp, bps = T_P // tm, DEC_SEQ // tm

    def imap(i):
        j = jnp.maximum(i - nbp, 0)
        return (layer * MOD_ROWS + jnp.where(i < nbp, 0, 1 + j // bps), 0, 0)
    return pl.BlockSpec((1, 1, N_MOD * D), imap)


def _rope_spec():
    def imap(i):
        j = jnp.maximum(i - NB_P, 0)
        return (jnp.where(i < NB_P, BPS, j % BPS), 0)
    return pl.BlockSpec((TM, LANE), imap)


def _seq_edges(i, tm=TM):
    row0 = i * tm
    seq_len = jnp.where(row0 < T_P, SEQ, DEC_SEQ)
    return row0 % seq_len != 0, (row0 + tm) % seq_len != 0


def _group_rms(xb, gain, group_ones):
    sq = xb * xb
    hi = sq.astype(BF16)
    lo = (sq - hi.astype(F32)).astype(BF16)
    ms = (_dot(hi, group_ones) + _dot(lo, group_ones)) * (1.0 / 64.0)
    return (xb * lax.rsqrt(ms + EPS)) * gain


def _rope128(xb, cos, sin, first):
    swapped = jnp.where(first, pltpu.roll(xb, LANE - 32, 1), pltpu.roll(xb, 32, 1))
    return xb * cos + swapped * sin


ADA_TN = 1536


def _adaln_kernel(cond_ref, w_ref, b_ref, o_ref):
    c = cond_ref[...]
    a = (c * _sigmoid(c)).astype(BF16)
    o_ref[0] = _dot(a, w_ref[0].astype(BF16)) + b_ref[0]


def _adaln(cond, ada_w, ada_b):
    n = N_MOD * D
    return pl.pallas_call(
        _adaln_kernel,
        out_shape=jax.ShapeDtypeStruct((DEPTH, MOD_ROWS, n), F32),
        grid=(DEPTH, n // ADA_TN),
        in_specs=[pl.BlockSpec((MOD_ROWS, D), lambda l, j: (0, 0)),
                  pl.BlockSpec((1, D, ADA_TN), lambda l, j: (l, 0, j)),
                  pl.BlockSpec((1, 1, ADA_TN), lambda l, j: (l, 0, j))],
        out_specs=pl.BlockSpec((1, MOD_ROWS, ADA_TN), lambda l, j: (l, 0, j)),
        compiler_params=_params(2),
        name="adaln",
    )(cond, ada_w, ada_b.reshape(DEPTH, 1, n))


MLA_DOWN_W = MLA_Q_LORA + MLA_KV_LORA + LANE


def _mla_in_kernel(xp_ref, xs_ref, mod_ref, g1_ref, wd_ref, qn_ref, kvn_ref, knr_ref, wuq_ref,
                   qnn_ref, qnr_ref, cos_ref, sin_ref, q_ref, ckv_ref, kr_ref):
    shift, scale = _mod_parts(mod_ref)[:2]
    h = _modulate(_split_rows(xp_ref, xs_ref), g1_ref[...], shift, scale).astype(BF16)
    d = _dot(h, wd_ref[...])
    cq = _rms(d[:, :MLA_Q_LORA], qn_ref[...]).astype(BF16)
    ckv_ref[...] = _rms(d[:, MLA_Q_LORA:MLA_Q_LORA + MLA_KV_LORA], kvn_ref[...])

    lane = lax.broadcasted_iota(jnp.int32, (1, LANE), 1)
    first = (lane % 64) < 32
    cos, sin = cos_ref[...], sin_ref[...]

    def rope_part(xb, gain):
        ms = jnp.sum(xb * xb, axis=-1, keepdims=True) * (1.0 / MLA_ROPE)
        return _rope128((xb * lax.rsqrt(ms + EPS)) * gain, cos, sin, first)

    kr_ref[...] = rope_part(d[:, MLA_Q_LORA + MLA_KV_LORA:], knr_ref[...])

    q = _dot(cq, wuq_ref[...])
    for hh in range(MLA_HEADS):
        c0 = hh * MLA_DK
        qn = _rms(q[:, c0:c0 + LANE], qnn_ref[...])
        qr = rope_part(q[:, c0 + LANE:c0 + MLA_DK], qnr_ref[...])
        q_ref[:, c0:c0 + LANE] = (qn * MLA_Q_SCALE).astype(BF16)
        q_ref[:, c0 + LANE:c0 + MLA_DK] = (qr * MLA_Q_SCALE).astype(BF16)


def _mla_in(xp, xs, mods, layer, g1, wd, qn, kvn, knr, wuq, qnn, qnr, cos_t, sin_t):
    return pl.pallas_call(
        _mla_in_kernel,
        out_shape=(jax.ShapeDtypeStruct((T, MLA_HEADS * MLA_DK), BF16),
                   jax.ShapeDtypeStruct((T, MLA_KV_LORA), F32),
                   jax.ShapeDtypeStruct((T, LANE), F32)),
        grid=(NB,),
        in_specs=[*_split_specs(D), _mod_spec(layer), _full((1, D)), _full((D, MLA_DOWN_W)),
                  _full((1, MLA_Q_LORA)), _full((1, MLA_KV_LORA)), _full((1, LANE)),
                  _full((MLA_Q_LORA, MLA_HEADS * MLA_DK)), _full((1, LANE)), _full((1, LANE)),
                  _rope_spec(), _rope_spec()],
        out_specs=(_row_spec(MLA_HEADS * MLA_DK), _row_spec(MLA_KV_LORA), _row_spec(LANE)),
        compiler_params=_params(),
        name="mla_in",
    )(xp, xs, mods, g1, wd, qn, kvn, knr, wuq, qnn, qnr, cos_t, sin_t)


def _mla_kv_kernel(ckv_ref, kr_ref, wuk_ref, wuv_ref, knn_ref, kt_ref, v_ref):
    c = ckv_ref[...].astype(BF16)
    kn = _dot(c, wuk_ref[...])
    v_ref[...] = _dot(c, wuv_ref[...]).astype(BF16)
    kr_t = kr_ref[...].T.astype(BF16)
    for hh in range(MLA_HEADS):
        r0 = hh * MLA_DK
        kh = _rms(kn[:, hh * LANE:(hh + 1) * LANE], knn_ref[...])
        kt_ref[r0:r0 + LANE, :] = kh.T.astype(BF16)
        kt_ref[r0 + LANE:r0 + MLA_DK, :] = kr_t


def _mla_kv(ckv, kr, wuk, wuv, knn):
    n = ckv.shape[0]
    return pl.pallas_call(
        _mla_kv_kernel,
        out_shape=(jax.ShapeDtypeStruct((MLA_HEADS * MLA_DK, n), BF16),
                   jax.ShapeDtypeStruct((n, MLA_HEADS * MLA_V), BF16)),
        grid=(n // TM,),
        in_specs=[_row_spec(MLA_KV_LORA), _row_spec(LANE),
                  _full((MLA_KV_LORA, MLA_HEADS * MLA_NOPE)), _full((MLA_KV_LORA, MLA_HEADS * MLA_V)),
                  _full((1, LANE))],
        out_specs=(pl.BlockSpec((MLA_HEADS * MLA_DK, TM), lambda i: (0, i)), _row_spec(MLA_HEADS * MLA_V)),
        compiler_params=_params(),
        name="mla_kv",
    )(ckv, kr, wuk, wuv, knn)


def _scores(q, kts, rows):
    parts = [_dot(q, kt_ref[rows, :]) for kt_ref in kts]
    return parts[0] if len(parts) == 1 else jnp.concatenate(parts, axis=1)


def _softmax_pv(s2, vs, cols):
    p = jnp.exp2(s2 - jnp.max(s2, axis=-1, keepdims=True)).astype(BF16)
    o, k0 = None, 0
    for v_ref in vs:
        v = v_ref[:, cols]
        part = _dot(p[:, k0:k0 + v.shape[0]], jnp.concatenate([v, jnp.ones_like(v)], axis=1))
        o = part if o is None else o + part
        k0 += v.shape[0]
    return o[:, :LANE] * (1.0 / o[:, LANE:LANE + 1])


def _mla_heads(q_ref, kts, vs, o_ref):
    for hh in range(MLA_HEADS):
        ks = slice(hh * MLA_DK, (hh + 1) * MLA_DK)
        cs = slice(hh * MLA_V, (hh + 1) * MLA_V)
        o_ref[:, cs] = _softmax_pv(_scores(q_ref[:, ks], kts, ks), vs, cs).astype(BF16)


def _diff_heads(lam_init, lq1_ref, lk1_ref, lq2_ref, lk2_ref, hn_ref, q_ref, kts, vs, o_ref):
    lam = (jnp.exp(jnp.sum(lq1_ref[...] * lk1_ref[...], axis=-1, keepdims=True))
           - jnp.exp(jnp.sum(lq2_ref[...] * lk2_ref[...], axis=-1, keepdims=True)) + lam_init)
    lane = lax.broadcasted_iota(jnp.int32, (1, LANE), 1)
    left = lane < DIFF_HD
    zero = jnp.zeros((), BF16)
    for hh in range(DIFF_HEADS):
        cs = slice(hh * LANE, (hh + 1) * LANE)
        qb = q_ref[:, cs]
        o0 = _softmax_pv(_scores(jnp.where(left, qb, zero), kts, cs), vs, cs)
        o1 = _softmax_pv(_scores(jnp.where(left, zero, qb), kts, cs), vs, cs)
        o = o0 - lam * o1
        o_ref[:, cs] = (_rms(o, hn_ref[...]) * (1.0 - lam_init)).astype(BF16)


def _residual_and_ffn_in(x, mix, mods, g2):
    _, _, gate1, shift2, scale2, _ = mods
    y = x + gate1 * mix
    return y, _modulate(y, g2, shift2, scale2).astype(BF16)


def _attn_kernel(heads, n_x, n_small, *refs):
    x_refs, refs = refs[:n_x], refs[n_x:]
    q_ref, ktp_ref, ktn_ref, ktc_ref, vp_ref, vn_ref, vc_ref = refs[:7]
    smalls, (mod_ref, wo_ref, g2_ref, y_ref, hf_ref, o_ref) = refs[7:7 + n_small], refs[7 + n_small:]
    is_prompt = pl.program_id(0) < NB_P

    @pl.when(is_prompt)
    def _():
        heads(*smalls, q_ref, [ktp_ref], [vp_ref], o_ref)

    @pl.when(jnp.logical_not(is_prompt))
    def _():
        heads(*smalls, q_ref, [ktc_ref, ktn_ref], [vc_ref, vn_ref], o_ref)

    x = x_refs[0][...] if n_x == 1 else _split_rows(*x_refs)
    y, hf = _residual_and_ffn_in(x, _dot(o_ref[...], wo_ref[...]), _mod_parts(mod_ref), g2_ref[...])
    y_ref[...] = y
    hf_ref[...] = hf


def _attention(heads, name, xs, q, kt, v, kt_cache, v_cache, smalls, mods, layer, wo, g2, dk, dv):
    assert SEQ == TM and T_P % DEC_SEQ == 0
    pro = lambda i: jnp.minimum(i, NB_P - 1)
    lat = lambda i: jnp.maximum(i - NB_P, 0) // BPS
    lat_new = lambda i: T_P // DEC_SEQ + lat(i)
    x_specs = [_row_spec(D)] if len(xs) == 1 else _split_specs(D)
    return pl.pallas_call(
        functools.partial(_attn_kernel, heads, len(xs), len(smalls)),
        out_shape=(jax.ShapeDtypeStruct((T, D), F32), jax.ShapeDtypeStruct((T, D), BF16)),
        grid=(NB,),
        in_specs=[*x_specs, _row_spec(dk),
                  pl.BlockSpec((dk, SEQ), lambda i: (0, pro(i))),
                  pl.BlockSpec((dk, DEC_SEQ), lambda i: (0, lat_new(i))),
                  pl.BlockSpec((dk, PAST), lambda i: (0, lat(i))),
                  pl.BlockSpec((SEQ, dv), lambda i: (pro(i), 0)),
                  pl.BlockSpec((DEC_SEQ, dv), lambda i: (lat_new(i), 0)),
                  pl.BlockSpec((PAST, dv), lambda i: (lat(i), 0)),
                  *[_full(s.shape) for s in smalls], _mod_spec(layer), _full((dv, D)), _full((1, D))],
        out_specs=(_row_spec(D), _row_spec(D)),
        scratch_shapes=[pltpu.VMEM((TM, dv), BF16)],
        compiler_params=_params(),
        name=name,
    )(*xs, q, kt, kt, kt_cache, v, v, v_cache, *smalls, mods, wo, g2)


def _zero_row(a, r, cond):
    r0 = r // HALO_F32 * HALO_F32
    tile = a[r0:r0 + HALO_F32]
    row = lax.broadcasted_iota(jnp.int32, (HALO_F32, 1), 0)
    tile = jnp.where(row == r - r0, jnp.where(cond, 0.0, tile), tile)
    return jnp.concatenate([a[:r0], tile, a[r0 + HALO_F32:]], axis=0)


def _conv3(zs_ref, w, off, tm=TM, is_prompt=False):
    z = zs_ref[...]
    n = z.shape[0]
    z_prev = pltpu.roll(z, 1, 0)[off:off + tm]
    z_next = pltpu.roll(z, n - 1, 0)[off:off + tm]
    for start in range(SEQ, tm, SEQ):
        z_prev = _zero_row(z_prev, start, is_prompt)
        z_next = _zero_row(z_next, start - 1, is_prompt)
    return z_prev * w[0:1] + z[off:off + tm] * w[1:2] + z_next * w[2:3]


FFN_TM = 512


def _ffn_kernel(n_out, y_ref, hf_ref, prev_ref, next_ref, mod_ref, win_ref, cw_ref, cb_ref, wout_ref,
                *rest):
    out_refs, (lhs_ref, z_ref, act_ref) = rest[:n_out], rest[n_out:]
    tm, hb = FFN_TM, HALO_BF16
    is_prompt = pl.program_id(0) < T_P // tm
    has_prev, has_next = _seq_edges(pl.program_id(0), tm)
    zeros = jnp.zeros((hb, D), BF16)
    lhs_ref[0:hb, :] = jnp.where(has_prev, prev_ref[...], zeros)
    lhs_ref[hb:hb + tm, :] = hf_ref[...]
    lhs_ref[hb + tm:, :] = jnp.where(has_next, next_ref[...], zeros)
    lhs = lhs_ref[...]
    for c in range(FFN_HIDDEN // HC):
        gs = slice(c * HC, (c + 1) * HC)
        us = slice(FFN_HIDDEN + c * HC, FFN_HIDDEN + (c + 1) * HC)
        z_ref[:, gs] = _dot(lhs, win_ref[0, :, gs])
        z_ref[:, us] = _dot(lhs, win_ref[0, :, us])
        g = _conv3(z_ref.at[:, gs], cw_ref[0, :, gs], hb, tm, is_prompt) + cb_ref[0, :, gs]
        u = _conv3(z_ref.at[:, us], cw_ref[0, :, us], hb, tm, is_prompt) + cb_ref[0, :, us]
        act_ref[:, gs] = ((g * _sigmoid(g)) * u).astype(BF16)
    gate2 = _mod_parts(mod_ref)[5]
    out = y_ref[...] + gate2 * _dot(act_ref[...], wout_ref[0])
    if len(out_refs) == 1:
        out_refs[0][...] = out
    else:
        @pl.when(is_prompt)
        def _():
            out_refs[0][...] = out

        @pl.when(jnp.logical_not(is_prompt))
        def _():
            out_refs[1][...] = out


def _ffn(y, hf, mods, layer, win, cw, cb, wout, split_out=False):
    tm, hb = FFN_TM, HALO_BF16
    r = tm // hb
    if split_out:
        out_shape = (jax.ShapeDtypeStruct((T_P, D), F32), jax.ShapeDtypeStruct((T_S, D), F32))
        out_specs = tuple(_split_specs(D, tm))
    else:
        out_shape, out_specs = (jax.ShapeDtypeStruct((T, D), F32),), (_row_spec(D, tm),)
    w_specs = [pl.BlockSpec((1,) + w.shape[1:], lambda i: (layer, 0, 0), pipeline_mode=pl.Buffered(1))
               for w in (win, cw, cb, wout)]
    return pl.pallas_call(
        functools.partial(_ffn_kernel, len(out_shape)),
        out_shape=out_shape,
        grid=(T // tm,),
        in_specs=[_row_spec(D, tm), _row_spec(D, tm),
                  pl.BlockSpec((hb, D), lambda i: (jnp.maximum(i * r - 1, 0), 0)),
                  pl.BlockSpec((hb, D), lambda i: (jnp.minimum((i + 1) * r, T // hb - 1), 0)),
                  _mod_spec(layer, tm), *w_specs],
        out_specs=out_specs,
        scratch_shapes=[pltpu.VMEM((tm + 2 * hb, D), BF16),
                        pltpu.VMEM((tm + 2 * hb, 2 * FFN_HIDDEN), F32),
                        pltpu.VMEM((tm, FFN_HIDDEN), BF16)],
        compiler_params=_params(),
        name="ffn",
    )(y, hf, hf, hf, mods, win, cw, cb, wout)


def _diff_in_kernel(x_ref, mod_ref, g1_ref, w_ref, qn_ref, kn_ref, cos_ref, sin_ref,
                    q_ref, kt_ref, v_ref, kst_ref, vst_ref, kn_ref_scratch):
    shift, scale = _mod_parts(mod_ref)[:2]
    h = _modulate(x_ref[...], g1_ref[...], shift, scale).astype(BF16)
    z = _dot(h, w_ref[...])
    lane = lax.broadcasted_iota(jnp.int32, (1, LANE), 1)
    first = (lane % 64) < 32
    same_half = (lax.broadcasted_iota(jnp.int32, (LANE, LANE), 0) < DIFF_HD) == (
        lax.broadcasted_iota(jnp.int32, (LANE, LANE), 1) < DIFF_HD)
    group_ones = jnp.where(same_half, 1.0, 0.0).astype(BF16)
    cos, sin = cos_ref[...], sin_ref[...]
    v = z[:, 2 * D:]
    v_ref[...] = v.astype(BF16)
    for hh in range(DIFF_HEADS):
        cs = slice(hh * LANE, (hh + 1) * LANE)
        qn = _group_rms(z[:, hh * LANE:(hh + 1) * LANE], qn_ref[...], group_ones)
        q_ref[:, cs] = (_rope128(qn, cos, sin, first) * DIFF_Q_SCALE).astype(BF16)
        kn = _group_rms(z[:, D + hh * LANE:D + (hh + 1) * LANE], kn_ref[...], group_ones)
        kt_ref[hh * LANE:(hh + 1) * LANE, :] = _rope128(kn, cos, sin, first).T.astype(BF16)
        kn_ref_scratch[:, cs] = kn

    @pl.when(pl.program_id(0) < NB_P)
    def _():
        vst_ref[...] = v
        kst_ref[...] = kn_ref_scratch[...]


def _diff_in(x, mods, layer, g1, w, qn, kn, cos_t, sin_t):
    st_spec = pl.BlockSpec((TM, D), lambda i: (jnp.minimum(i, NB_P - 1), 0))
    return pl.pallas_call(
        _diff_in_kernel,
        out_shape=(jax.ShapeDtypeStruct((T, D), BF16), jax.ShapeDtypeStruct((D, T), BF16),
                   jax.ShapeDtypeStruct((T, D), BF16), jax.ShapeDtypeStruct((T_P, D), F32),
                   jax.ShapeDtypeStruct((T_P, D), F32)),
        grid=(NB,),
        in_specs=[_row_spec(D), _mod_spec(layer), _full((1, D)), _full((D, 3 * D)),
                  _full((1, LANE)), _full((1, LANE)), _rope_spec(), _rope_spec()],
        out_specs=(_row_spec(D), pl.BlockSpec((D, TM), lambda i: (0, i)), _row_spec(D), st_spec, st_spec),
        scratch_shapes=[pltpu.VMEM((TM, D), F32)],
        compiler_params=_params(),
        name="diff_in",
    )(x, mods, g1, w, qn, kn, cos_t, sin_t)


def _sconv_kernel(x_ref, prev_ref, next_ref, mod_ref, g1_ref, win_ref, cw_ref, wout_ref, g2_ref,
                  y_ref, hf_ref, p_ref, mixed_ref):
    has_prev, has_next = _seq_edges(pl.program_id(0))
    hb = HALO_F32
    mods = _mod_parts(mod_ref)
    x = x_ref[...]
    mod_in = lambda a: _modulate(a, g1_ref[...], mods[0], mods[1])
    zeros = jnp.zeros((hb, D), F32)
    lhs = jnp.concatenate([jnp.where(has_prev, mod_in(prev_ref[...]), zeros), mod_in(x),
                           jnp.where(has_next, mod_in(next_ref[...]), zeros)],
                          axis=0).astype(BF16)
    for c in range(D // HC):
        cs = slice(c * HC, (c + 1) * HC)
        gb = _dot(lhs, win_ref[:, c * HC:(c + 1) * HC])
        gc = _dot(lhs, win_ref[:, D + c * HC:D + (c + 1) * HC])
        u = _dot(lhs, win_ref[:, 2 * D + c * HC:2 * D + (c + 1) * HC])
        p_ref[:, cs] = gc * u
        mixed_ref[:, cs] = (gb[hb:hb + TM] * _conv3(p_ref.at[:, cs], cw_ref[:, cs], hb)).astype(BF16)
    y, hf = _residual_and_ffn_in(x, _dot(mixed_ref[...], wout_ref[...]), mods, g2_ref[...])
    y_ref[...] = y
    hf_ref[...] = hf


def _sconv(x, mods, layer, g1, win, cw, wout, g2):
    hb = HALO_F32
    r = TM // hb
    return pl.pallas_call(
        _sconv_kernel,
        out_shape=(jax.ShapeDtypeStruct((T, D), F32), jax.ShapeDtypeStruct((T, D), BF16)),
        grid=(NB,),
        in_specs=[_row_spec(D),
                  pl.BlockSpec((hb, D), lambda i: (jnp.maximum(i * r - 1, 0), 0)),
                  pl.BlockSpec((hb, D), lambda i: (jnp.minimum((i + 1) * r, T // hb - 1), 0)),
                  _mod_spec(layer), _full((1, D)), _full((D, 3 * D)), _full((3, D)),
                  _full((D, D)), _full((1, D))],
        out_specs=(_row_spec(D), _row_spec(D)),
        scratch_shapes=[pltpu.VMEM((TM + 2 * hb, D), F32), pltpu.VMEM((TM, D), BF16)],
        compiler_params=_params(),
        name="sconv",
    )(x, x, x, mods, g1, win, cw, wout, g2)


def _gmlp_kernel(x_ref, mod_ref, g1_ref, win_ref, vn_ref, ws_ref, bs_ref, wout_ref, g2_ref,
                 y_ref, hf_ref, gated_ref):
    mods = _mod_parts(mod_ref)
    x = x_ref[...]
    h = _modulate(x, g1_ref[...], mods[0], mods[1]).astype(BF16)
    z = _dot(h, win_ref[...])
    z = z * (0.5 * (1.0 + jnp.tanh(math.sqrt(2.0 / math.pi) * (z + 0.044715 * (z * z * z)))))
    u = z[:, :D]
    v = _rms(z[:, D:], vn_ref[...]).astype(BF16)
    bs = bs_ref[...]
    for g in range(GMLP_GROUPS):
        cs = slice(g * LANE, (g + 1) * LANE)
        w = ws_ref[g]
        for r in range(TM // GMLP_CHUNK):
            rs = slice(r * GMLP_CHUNK, (r + 1) * GMLP_CHUNK)
            mixed = _dot(w, v[rs, cs]) + bs[:, g:g + 1]
            gated_ref[rs, cs] = (u[rs, cs] * mixed).astype(BF16)
    y, hf = _residual_and_ffn_in(x, _dot(gated_ref[...], wout_ref[...]), mods, g2_ref[...])
    y_ref[...] = y
    hf_ref[...] = hf


def _gmlp(x, mods, layer, g1, win, vn, ws, bs_t, wout, g2):
    return pl.pallas_call(
        _gmlp_kernel,
        out_shape=(jax.ShapeDtypeStruct((T, D), F32), jax.ShapeDtypeStruct((T, D), BF16)),
        grid=(NB,),
        in_specs=[_row_spec(D), _mod_spec(layer), _full((1, D)), _full((D, 2 * D)), _full((1, D)),
                  _full((GMLP_GROUPS, GMLP_CHUNK, GMLP_CHUNK)), _full((GMLP_CHUNK, GMLP_GROUPS)),
                  _full((D, D)), _full((1, D))],
        out_specs=(_row_spec(D), _row_spec(D)),
        scratch_shapes=[pltpu.VMEM((TM, D), BF16)],
        compiler_params=_params(),
        name="gmlp",
    )(x, mods, g1, win, vn, ws, bs_t, wout, g2)


def _rope_tables():
    rows = DEC_SEQ // GRID_W
    row = jnp.repeat(jnp.arange(rows, dtype=F32), GRID_W)
    col = jnp.tile(jnp.arange(GRID_W, dtype=F32), rows)
    n_freq = MLA_ROPE // 4
    inv_freq = ROPE_THETA ** (-jnp.arange(n_freq, dtype=F32) / n_freq)
    ang = jnp.concatenate([row[:, None] * inv_freq, col[:, None] * inv_freq], axis=-1)
    cos, sin = jnp.cos(ang), jnp.sin(ang)
    cos = jnp.concatenate([cos, jnp.ones((TM, cos.shape[1]), F32)], axis=0)
    sin = jnp.concatenate([sin, jnp.zeros((TM, sin.shape[1]), F32)], axis=0)
    return jnp.tile(cos, (1, 4)), jnp.tile(jnp.concatenate([-sin, sin], axis=1), (1, 2))


def _pad_lanes(a, width=LANE):
    return jnp.pad(a, [(0, 0)] * (a.ndim - 1) + [(0, width - a.shape[-1])])


def _heads_first(a):
    return jnp.swapaxes(a, -3, -2).reshape(a.shape[:-3] + (D,))


def kernel(x_prompt, x_sample, cache_mla_ckv, cache_mla_krope, cache_diff_k, cache_diff_v, c, c_ctx, ada_w, ada_b, norm1_g, norm2_g, mla_w_down, mla_q_norm, mla_kv_norm, mla_w_uq, mla_w_uk, mla_w_uv, mla_qn_nope, mla_qn_rope, mla_kn_nope, mla_kn_rope, mla_w_o, diff_w_qkv, diff_qn, diff_kn, diff_lq1, diff_lk1, diff_lq2, diff_lk2, diff_head_norm, diff_w_o, sconv_w_in, sconv_w, sconv_w_out, gmlp_w_in, gmlp_v_norm, gmlp_w_s, gmlp_b_s, gmlp_w_out, ffn_w_in, ffn_conv_w, ffn_conv_b, ffn_w_out):
    xp, xs = x_prompt.reshape(T_P, D), x_sample.reshape(T_S, D)
    cond = jnp.concatenate([c_ctx[None], c, jnp.zeros((MOD_ROWS - 1 - DEC_BATCH, D), F32)], axis=0)
    mods = _adaln(cond, ada_w, ada_b).reshape(DEPTH * MOD_ROWS, 1, N_MOD * D)
    cos_t, sin_t = _rope_tables()
    row = lambda a: a.reshape(1, -1)

    ffn_w = (ffn_w_in.astype(BF16), ffn_conv_w, ffn_conv_b[:, None, :], ffn_w_out.astype(BF16))

    def ffn(y, hf, i):
        return _ffn(y, hf, mods, i, *ffn_w, split_out=(i == DEPTH - 1))

    wd = _pad_lanes(mla_w_down[0], MLA_DOWN_W).astype(BF16)
    wuq = _pad_lanes(mla_w_uq[0].reshape(MLA_Q_LORA, MLA_HEADS, MLA_NOPE + MLA_ROPE), MLA_DK)
    wuq = wuq.reshape(MLA_Q_LORA, MLA_HEADS * MLA_DK).astype(BF16)
    q, ckv, kr = _mla_in(xp, xs, mods, 0, row(norm1_g[0]), wd, row(mla_q_norm[0]), row(mla_kv_norm[0]),
                         _pad_lanes(row(mla_kn_rope[0])), wuq, row(mla_qn_nope[0]),
                         _pad_lanes(row(mla_qn_rope[0])), cos_t, sin_t)
    state_ckv = ckv[:T_P].reshape(BATCH, 1, SEQ, MLA_KV_LORA)
    state_kr = kr[:T_P, :MLA_ROPE].reshape(BATCH, 1, SEQ, MLA_ROPE)
    kv_w = (mla_w_uk[0].astype(BF16), mla_w_uv[0].astype(BF16), row(mla_kn_nope[0]))
    kt, v = _mla_kv(ckv, kr, *kv_w)
    kt_cache, v_cache = _mla_kv(cache_mla_ckv[:, 0].reshape(DEC_BATCH * PAST, MLA_KV_LORA),
                                _pad_lanes(cache_mla_krope[:, 0]).reshape(DEC_BATCH * PAST, LANE), *kv_w)
    y, hf = _attention(_mla_heads, "mla_attn", (xp, xs), q, kt, v, kt_cache, v_cache, [], mods, 0,
                       mla_w_o[0].astype(BF16), row(norm2_g[0]), MLA_HEADS * MLA_DK, MLA_HEADS * MLA_V)
    x, = ffn(y, hf, 0)

    lam_init = 0.8 - 0.6 * math.exp(-0.3 * 1)
    wq, wk, wv = jnp.split(diff_w_qkv[0], 3, axis=-1)
    perm = lambda w: _heads_first(w.reshape(D, 2, DIFF_HEADS, DIFF_HD))
    w_qkv = jnp.concatenate([perm(wq), perm(wk), wv], axis=-1).astype(BF16)
    pair = lambda g: row(jnp.tile(g, 2))
    q, kt, v, kst, vst = _diff_in(x, mods, 1, row(norm1_g[1]), w_qkv, pair(diff_qn[0]), pair(diff_kn[0]),
                                 cos_t, sin_t)
    state_dk = jnp.swapaxes(kst.reshape(BATCH, SEQ, DIFF_HEADS, 2, DIFF_HD), 2, 3)[:, None]
    state_dv = vst.reshape(BATCH, 1, SEQ, DIFF_HEADS, 2 * DIFF_HD)
    kt_cache = _heads_first(cache_diff_k[:, 0]).astype(BF16).reshape(DEC_BATCH * PAST, D).T
    v_cache = cache_diff_v[:, 0].reshape(DEC_BATCH * PAST, D).astype(BF16)
    smalls = [row(diff_lq1[0]), row(diff_lk1[0]), row(diff_lq2[0]), row(diff_lk2[0]),
              row(diff_head_norm[0])]
    y, hf = _attention(functools.partial(_diff_heads, lam_init), "diff_attn", (x,), q, kt, v, kt_cache, v_cache,
                       smalls, mods, 1, diff_w_o[0].astype(BF16), row(norm2_g[1]), D, D)
    x, = ffn(y, hf, 1)

    y, hf = _sconv(x, mods, 2, row(norm1_g[2]), sconv_w_in[0].astype(BF16), sconv_w[0],
                   sconv_w_out[0].astype(BF16), row(norm2_g[2]))
    x, = ffn(y, hf, 2)

    y, hf = _gmlp(x, mods, 3, row(norm1_g[3]), gmlp_w_in[0].astype(BF16), row(gmlp_v_norm[0]),
                  gmlp_w_s[0].astype(BF16), gmlp_b_s[0].T, gmlp_w_out[0].astype(BF16), row(norm2_g[3]))
    yp, ys = ffn(y, hf, 3)

    return (yp.reshape(BATCH, SEQ, D), ys.reshape(DEC_BATCH, DEC_SEQ, D),
            state_ckv, state_kr, state_dk, state_dv)
```

```python
import functools
import math

import jax
import jax.numpy as jnp
from jax import lax
from jax.experimental import pallas as pl
from jax.experimental.pallas import tpu as pltpu

D = 1024
BATCH, SEQ = 16, 256
DEC_BATCH, DEC_SEQ = 8, 2048
PAST = 256
DEPTH = 4
GRID_W = 64
EPS = 1e-6
ROPE_THETA = 10000.0
N_MOD = 6
MLA_HEADS, MLA_NOPE, MLA_ROPE, MLA_V = 8, 128, 64, 128
MLA_Q_LORA, MLA_KV_LORA = 768, 256
DIFF_HEADS, DIFF_HD = 8, 64
GMLP_CHUNK, GMLP_GROUPS = 128, 8
FFN_HIDDEN = 2816

T_P = BATCH * SEQ
T_S = DEC_BATCH * DEC_SEQ
T = T_P + T_S
TM = 256
TMX = 512
FFN_TM = 512
NB_P = T_P // TM
NB = T // TM
BPS = DEC_SEQ // TM
KV_LEN = PAST + DEC_SEQ
MOD_ROWS = 16
LANE = 128
MLA_DK = 256
HC = 256
HALO_BF16 = 16
HALO_F32 = 8
VMEM_LIMIT = 56 * 1024 * 1024
LOG2E = math.log2(math.e)
MLA_Q_SCALE = (MLA_NOPE + MLA_ROPE) ** -0.5 * LOG2E
DIFF_Q_SCALE = DIFF_HD ** -0.5 * LOG2E

F32 = jnp.float32
BF16 = jnp.bfloat16


def _params(n_axes=1, vmem=VMEM_LIMIT):
    return pltpu.CompilerParams(dimension_semantics=("arbitrary",) * n_axes,
                                vmem_limit_bytes=vmem)


def _dot(a, b):
    return jnp.dot(a, b, preferred_element_type=F32)


def _rms(x, g):
    ms = jnp.mean(x * x, axis=-1, keepdims=True)
    return (x * lax.rsqrt(ms + EPS)) * g


def _modulate(x, g, shift, scale):
    return _rms(x, g) * (1.0 + scale) + shift


def _sigmoid(x):
    return 1.0 / (1.0 + jnp.exp(-x))


def _mod_parts(mod_ref):
    m = mod_ref[0]
    return [m[:, k * D:(k + 1) * D] for k in range(N_MOD)]


def _full(shape):
    return pl.BlockSpec(shape, lambda *_: (0,) * len(shape))


def _row_spec(width, rows=TM):
    return pl.BlockSpec((rows, width), lambda i: (i, 0))


def _split_specs(width, tm=TM):
    nbp = T_P // tm
    return [pl.BlockSpec((tm, width), lambda i: (jnp.minimum(i, nbp - 1), 0)),
            pl.BlockSpec((tm, width), lambda i: (jnp.maximum(i - nbp, 0), 0))]


def _split_rows(p_ref, s_ref, tm=TM):
    return jnp.where(pl.program_id(0) < T_P // tm, p_ref[...], s_ref[...])


def _mod_spec(layer, tm=TM):
    nbp, bps = T_P // tm, DEC_SEQ // tm

    def imap(i):
        j = jnp.maximum(i - nbp, 0)
        return (layer * MOD_ROWS + jnp.where(i < nbp, 0, 1 + j // bps), 0, 0)
    return pl.BlockSpec((1, 1, N_MOD * D), imap)


def _rope_spec(tm=TMX):
    nbp, bps = T_P // tm, DEC_SEQ // tm

    def imap(i):
        j = jnp.maximum(i - nbp, 0)
        return (jnp.where(i < nbp, bps, j % bps), 0)
    return pl.BlockSpec((tm, LANE), imap)


def _seq_edges(i, tm=TM):
    row0 = i * tm
    seq_len = jnp.where(row0 < T_P, SEQ, DEC_SEQ)
    return row0 % seq_len != 0, (row0 + tm) % seq_len != 0


def _group_rms(xb, gain, group_ones):
    sq = xb * xb
    hi = sq.astype(BF16)
    lo = (sq - hi.astype(F32)).astype(BF16)
    ms = (_dot(hi, group_ones) + _dot(lo, group_ones)) * (1.0 / 64.0)
    return (xb * lax.rsqrt(ms + EPS)) * gain


def _rope128(xb, cos, sin, first):
    swapped = jnp.where(first, pltpu.roll(xb, LANE - 32, 1), pltpu.roll(xb, 32, 1))
    return xb * cos + swapped * sin


ADA_TN = 1536


def _adaln_kernel(cond_ref, w_ref, b_ref, o_ref):
    c = cond_ref[...]
    a = (c * _sigmoid(c)).astype(BF16)
    o_ref[0] = _dot(a, w_ref[0].astype(BF16)) + b_ref[0]


def _adaln(cond, ada_w, ada_b):
    n = N_MOD * D
    return pl.pallas_call(
        _adaln_kernel,
        out_shape=jax.ShapeDtypeStruct((DEPTH, MOD_ROWS, n), F32),
        grid=(DEPTH, n // ADA_TN),
        in_specs=[pl.BlockSpec((MOD_ROWS, D), lambda l, j: (0, 0)),
                  pl.BlockSpec((1, D, ADA_TN), lambda l, j: (l, 0, j)),
                  pl.BlockSpec((1, 1, ADA_TN), lambda l, j: (l, 0, j))],
        out_specs=pl.BlockSpec((1, MOD_ROWS, ADA_TN), lambda l, j: (l, 0, j)),
        compiler_params=_params(2),
        name="adaln",
    )(cond, ada_w, ada_b.reshape(DEPTH, 1, n))


MLA_DOWN_W = MLA_Q_LORA + MLA_KV_LORA + LANE


def _mla_in_kernel(xp_ref, xs_ref, mod_ref, g1_ref, wd_ref, qn_ref, kvn_ref, knr_ref, wuq_ref,
                   qnn_ref, qnr_ref, cos_ref, sin_ref, q_ref, ckv_ref, kr_ref):
    shift, scale = _mod_parts(mod_ref)[:2]
    h = _modulate(_split_rows(xp_ref, xs_ref), g1_ref[...], shift, scale).astype(BF16)
    d = _dot(h, wd_ref[...])
    cq = _rms(d[:, :MLA_Q_LORA], qn_ref[...]).astype(BF16)
    ckv_ref[...] = _rms(d[:, MLA_Q_LORA:MLA_Q_LORA + MLA_KV_LORA], kvn_ref[...])

    lane = lax.broadcasted_iota(jnp.int32, (1, LANE), 1)
    first = (lane % 64) < 32
    cos, sin = cos_ref[...], sin_ref[...]

    def rope_part(xb, gain):
        ms = jnp.sum(xb * xb, axis=-1, keepdims=True) * (1.0 / MLA_ROPE)
        return _rope128((xb * lax.rsqrt(ms + EPS)) * gain, cos, sin, first)

    kr_ref[...] = rope_part(d[:, MLA_Q_LORA + MLA_KV_LORA:], knr_ref[...])

    q = _dot(cq, wuq_ref[...])
    for hh in range(MLA_HEADS):
        c0 = hh * MLA_DK
        qn = _rms(q[:, c0:c0 + LANE], qnn_ref[...])
        qr = rope_part(q[:, c0 + LANE:c0 + MLA_DK], qnr_ref[...])
        q_ref[:, c0:c0 + LANE] = (qn * MLA_Q_SCALE).astype(BF16)
        q_ref[:, c0 + LANE:c0 + MLA_DK] = (qr * MLA_Q_SCALE).astype(BF16)


def _mla_in(xp, xs, mods, layer, g1, wd, qn, kvn, knr, wuq, qnn, qnr, cos_t, sin_t):
    return pl.pallas_call(
        _mla_in_kernel,
        out_shape=(jax.ShapeDtypeStruct((T, MLA_HEADS * MLA_DK), BF16),
                   jax.ShapeDtypeStruct((T, MLA_KV_LORA), F32),
                   jax.ShapeDtypeStruct((T, LANE), F32)),
        grid=(NB,),
        in_specs=[*_split_specs(D), _mod_spec(layer), _full((1, D)), _full((D, MLA_DOWN_W)),
                  _full((1, MLA_Q_LORA)), _full((1, MLA_KV_LORA)), _full((1, LANE)),
                  _full((MLA_Q_LORA, MLA_HEADS * MLA_DK)), _full((1, LANE)), _full((1, LANE)),
                  _rope_spec(TM), _rope_spec(TM)],
        out_specs=(_row_spec(MLA_HEADS * MLA_DK), _row_spec(MLA_KV_LORA), _row_spec(LANE)),
        compiler_params=_params(),
        name="mla_in",
    )(xp, xs, mods, g1, wd, qn, kvn, knr, wuq, qnn, qnr, cos_t, sin_t)


def _mla_kv_kernel(ckv_ref, kr_ref, wuk_ref, wuv_ref, knn_ref, kt_ref, v_ref):
    c = ckv_ref[...].astype(BF16)
    kn = _dot(c, wuk_ref[...])
    v_ref[...] = _dot(c, wuv_ref[...]).astype(BF16)
    kr_t = kr_ref[...].T.astype(BF16)
    for hh in range(MLA_HEADS):
        r0 = hh * MLA_DK
        kh = _rms(kn[:, hh * LANE:(hh + 1) * LANE], knn_ref[...])
        kt_ref[r0:r0 + LANE, :] = kh.T.astype(BF16)
        kt_ref[r0 + LANE:r0 + MLA_DK, :] = kr_t


def _mla_kv(ckv, kr, wuk, wuv, knn):
    n = ckv.shape[0]
    return pl.pallas_call(
        _mla_kv_kernel,
        out_shape=(jax.ShapeDtypeStruct((MLA_HEADS * MLA_DK, n), BF16),
                   jax.ShapeDtypeStruct((n, MLA_HEADS * MLA_V), BF16)),
        grid=(n // TMX,),
        in_specs=[_row_spec(MLA_KV_LORA, TMX), _row_spec(LANE, TMX),
                  _full((MLA_KV_LORA, MLA_HEADS * MLA_NOPE)), _full((MLA_KV_LORA, MLA_HEADS * MLA_V)),
                  _full((1, LANE))],
        out_specs=(pl.BlockSpec((MLA_HEADS * MLA_DK, TMX), lambda i: (0, i)),
                   _row_spec(MLA_HEADS * MLA_V, TMX)),
        compiler_params=_params(),
        name="mla_kv",
    )(ckv, kr, wuk, wuv, knn)


def _scores(q, kts, rows):
    parts = [_dot(q, kt_ref[rows, :]) for kt_ref in kts]
    return parts[0] if len(parts) == 1 else jnp.concatenate(parts, axis=1)


def _softmax_pv(s2, vs, cols):
    p = jnp.exp2(s2 - jnp.max(s2, axis=-1, keepdims=True)).astype(BF16)
    o, k0 = None, 0
    for v_ref in vs:
        v = v_ref[:, cols]
        part = _dot(p[:, k0:k0 + v.shape[0]], jnp.concatenate([v, jnp.ones_like(v)], axis=1))
        o = part if o is None else o + part
        k0 += v.shape[0]
    return o[:, :LANE] * (1.0 / o[:, LANE:LANE + 1])


def _mla_heads(q_ref, kts, vs, o_ref):
    for hh in range(MLA_HEADS):
        ks = slice(hh * MLA_DK, (hh + 1) * MLA_DK)
        cs = slice(hh * MLA_V, (hh + 1) * MLA_V)
        o_ref[:, cs] = _softmax_pv(_scores(q_ref[:, ks], kts, ks), vs, cs).astype(BF16)


def _diff_heads(lam_init, lq1_ref, lk1_ref, lq2_ref, lk2_ref, hn_ref, q_ref, kts, vs, o_ref):
    lam = (jnp.exp(jnp.sum(lq1_ref[...] * lk1_ref[...], axis=-1, keepdims=True))
           - jnp.exp(jnp.sum(lq2_ref[...] * lk2_ref[...], axis=-1, keepdims=True)) + lam_init)
    lane = lax.broadcasted_iota(jnp.int32, (1, LANE), 1)
    left = lane < DIFF_HD
    zero = jnp.zeros((), BF16)
    for hh in range(DIFF_HEADS):
        cs = slice(hh * LANE, (hh + 1) * LANE)
        qb = q_ref[:, cs]
        o0 = _softmax_pv(_scores(jnp.where(left, qb, zero), kts, cs), vs, cs)
        o1 = _softmax_pv(_scores(jnp.where(left, zero, qb), kts, cs), vs, cs)
        o = o0 - lam * o1
        o_ref[:, cs] = (_rms(o, hn_ref[...]) * (1.0 - lam_init)).astype(BF16)


def _residual_and_ffn_in(x, mix, mods, g2):
    _, _, gate1, shift2, scale2, _ = mods
    y = x + gate1 * mix
    return y, _modulate(y, g2, shift2, scale2).astype(BF16)


def _attn_kernel(heads, n_x, n_small, *refs):
    x_refs, refs = refs[:n_x], refs[n_x:]
    q_ref, ktp_ref, ktn_ref, ktc_ref, vp_ref, vn_ref, vc_ref = refs[:7]
    smalls, (mod_ref, wo_ref, g2_ref, y_ref, hf_ref, o_ref) = refs[7:7 + n_small], refs[7 + n_small:]
    is_prompt = pl.program_id(0) < NB_P

    @pl.when(is_prompt)
    def _():
        heads(*smalls, q_ref, [ktp_ref], [vp_ref], o_ref)

    @pl.when(jnp.logical_not(is_prompt))
    def _():
        heads(*smalls, q_ref, [ktc_ref, ktn_ref], [vc_ref, vn_ref], o_ref)

    x = x_refs[0][...] if n_x == 1 else _split_rows(*x_refs)
    y, hf = _residual_and_ffn_in(x, _dot(o_ref[...], wo_ref[...]), _mod_parts(mod_ref), g2_ref[...])
    y_ref[...] = y
    hf_ref[...] = hf


def _attention(heads, name, xs, q, kt, v, kt_cache, v_cache, smalls, mods, layer, wo, g2, dk, dv):
    assert SEQ == TM and T_P % DEC_SEQ == 0
    pro = lambda i: jnp.minimum(i, NB_P - 1)
    lat = lambda i: jnp.maximum(i - NB_P, 0) // BPS
    lat_new = lambda i: T_P // DEC_SEQ + lat(i)
    x_specs = [_row_spec(D)] if len(xs) == 1 else _split_specs(D)
    return pl.pallas_call(
        functools.partial(_attn_kernel, heads, len(xs), len(smalls)),
        out_shape=(jax.ShapeDtypeStruct((T, D), F32), jax.ShapeDtypeStruct((T, D), BF16)),
        grid=(NB,),
        in_specs=[*x_specs, _row_spec(dk),
                  pl.BlockSpec((dk, SEQ), lambda i: (0, pro(i))),
                  pl.BlockSpec((dk, DEC_SEQ), lambda i: (0, lat_new(i))),
                  pl.BlockSpec((dk, PAST), lambda i: (0, lat(i))),
                  pl.BlockSpec((SEQ, dv), lambda i: (pro(i), 0)),
                  pl.BlockSpec((DEC_SEQ, dv), lambda i: (lat_new(i), 0)),
                  pl.BlockSpec((PAST, dv), lambda i: (lat(i), 0)),
                  *[_full(s.shape) for s in smalls], _mod_spec(layer), _full((dv, D)), _full((1, D))],
        out_specs=(_row_spec(D), _row_spec(D)),
        scratch_shapes=[pltpu.VMEM((TM, dv), BF16)],
        compiler_params=_params(),
        name=name,
    )(*xs, q, kt, kt, kt_cache, v, v, v_cache, *smalls, mods, wo, g2)


def _zero_row(a, r, cond):
    r0 = r // HALO_F32 * HALO_F32
    tile = a[r0:r0 + HALO_F32]
    row = lax.broadcasted_iota(jnp.int32, (HALO_F32, 1), 0)
    tile = jnp.where(row == r - r0, jnp.where(cond, 0.0, tile), tile)
    return jnp.concatenate([a[:r0], tile, a[r0 + HALO_F32:]], axis=0)


def _conv3(zs_ref, w, off, tm=TM, is_prompt=False):
    z = zs_ref[...]
    n = z.shape[0]
    z_prev = pltpu.roll(z, 1, 0)[off:off + tm]
    z_next = pltpu.roll(z, n - 1, 0)[off:off + tm]
    for start in range(SEQ, tm, SEQ):
        z_prev = _zero_row(z_prev, start, is_prompt)
        z_next = _zero_row(z_next, start - 1, is_prompt)
    return z_prev * w[0:1] + z[off:off + tm] * w[1:2] + z_next * w[2:3]


def _ffn_kernel(n_out, y_ref, hf_ref, prev_ref, next_ref, mod_ref, win_ref, cw_ref, cb_ref, wout_ref,
                *rest):
    out_refs, (lhs_ref, z_ref, act_ref) = rest[:n_out], rest[n_out:]
    tm, hb = FFN_TM, HALO_BF16
    is_prompt = pl.program_id(0) < T_P // tm
    has_prev, has_next = _seq_edges(pl.program_id(0), tm)
    zeros = jnp.zeros((hb, D), BF16)
    lhs_ref[0:hb, :] = jnp.where(has_prev, prev_ref[...], zeros)
    lhs_ref[hb:hb + tm, :] = hf_ref[...]
    lhs_ref[hb + tm:, :] = jnp.where(has_next, next_ref[...], zeros)
    lhs = lhs_ref[...]
    for c in range(FFN_HIDDEN // HC):
        gs = slice(c * HC, (c + 1) * HC)
        us = slice(FFN_HIDDEN + c * HC, FFN_HIDDEN + (c + 1) * HC)
        z_ref[:, gs] = _dot(lhs, win_ref[0, :, gs])
        z_ref[:, us] = _dot(lhs, win_ref[0, :, us])
        g = _conv3(z_ref.at[:, gs], cw_ref[0, :, gs], hb, tm, is_prompt) + cb_ref[0, :, gs]
        u = _conv3(z_ref.at[:, us], cw_ref[0, :, us], hb, tm, is_prompt) + cb_ref[0, :, us]
        act_ref[:, gs] = ((g * _sigmoid(g)) * u).astype(BF16)
    gate2 = _mod_parts(mod_ref)[5]
    out = y_ref[...] + gate2 * _dot(act_ref[...], wout_ref[0])
    if len(out_refs) == 1:
        out_refs[0][...] = out
    else:
        @pl.when(is_prompt)
        def _():
            out_refs[0][...] = out

        @pl.when(jnp.logical_not(is_prompt))
        def _():
            out_refs[1][...] = out


def _ffn(y, hf, mods, layer, win, cw, cb, wout, split_out=False):
    tm, hb = FFN_TM, HALO_BF16
    r = tm // hb
    if split_out:
        out_shape = (jax.ShapeDtypeStruct((T_P, D), F32), jax.ShapeDtypeStruct((T_S, D), F32))
        out_specs = tuple(_split_specs(D, tm))
    else:
        out_shape, out_specs = (jax.ShapeDtypeStruct((T, D), F32),), (_row_spec(D, tm),)
    w_specs = [pl.BlockSpec((1,) + w.shape[1:], lambda i: (layer, 0, 0), pipeline_mode=pl.Buffered(1))
               for w in (win, cw, cb, wout)]
    return pl.pallas_call(
        functools.partial(_ffn_kernel, len(out_shape)),
        out_shape=out_shape,
        grid=(T // tm,),
        in_specs=[_row_spec(D, tm), _row_spec(D, tm),
                  pl.BlockSpec((hb, D), lambda i: (jnp.maximum(i * r - 1, 0), 0)),
                  pl.BlockSpec((hb, D), lambda i: (jnp.minimum((i + 1) * r, T // hb - 1), 0)),
                  _mod_spec(layer, tm), *w_specs],
        out_specs=out_specs,
        scratch_shapes=[pltpu.VMEM((tm + 2 * hb, D), BF16),
                        pltpu.VMEM((tm + 2 * hb, 2 * FFN_HIDDEN), F32),
                        pltpu.VMEM((tm, FFN_HIDDEN), BF16)],
        compiler_params=_params(),
        name="ffn",
    )(y, hf, hf, hf, mods, win, cw, cb, wout)


def _diff_in_kernel(x_ref, mod_ref, g1_ref, w_ref, qn_ref, kn_ref, cos_ref, sin_ref,
                    q_ref, kt_ref, v_ref, kst_ref, vst_ref, kn_ref_scratch):
    shift, scale = _mod_parts(mod_ref)[:2]
    h = _modulate(x_ref[...], g1_ref[...], shift, scale).astype(BF16)
    z = _dot(h, w_ref[...])
    lane = lax.broadcasted_iota(jnp.int32, (1, LANE), 1)
    first = (lane % 64) < 32
    same_half = (lax.broadcasted_iota(jnp.int32, (LANE, LANE), 0) < DIFF_HD) == (
        lax.broadcasted_iota(jnp.int32, (LANE, LANE), 1) < DIFF_HD)
    group_ones = jnp.where(same_half, 1.0, 0.0).astype(BF16)
    cos, sin = cos_ref[...], sin_ref[...]
    v = z[:, 2 * D:]
    v_ref[...] = v.astype(BF16)
    for hh in range(DIFF_HEADS):
        cs = slice(hh * LANE, (hh + 1) * LANE)
        qn = _group_rms(z[:, hh * LANE:(hh + 1) * LANE], qn_ref[...], group_ones)
        q_ref[:, cs] = (_rope128(qn, cos, sin, first) * DIFF_Q_SCALE).astype(BF16)
        kn = _group_rms(z[:, D + hh * LANE:D + (hh + 1) * LANE], kn_ref[...], group_ones)
        kt_ref[hh * LANE:(hh + 1) * LANE, :] = _rope128(kn, cos, sin, first).T.astype(BF16)
        kn_ref_scratch[:, cs] = kn

    @pl.when(pl.program_id(0) < T_P // TMX)
    def _():
        vst_ref[...] = v
        kst_ref[...] = kn_ref_scratch[...]


def _diff_in(x, mods, layer, g1, w, qn, kn, cos_t, sin_t):
    st_spec = _split_specs(D, TMX)[0]
    return pl.pallas_call(
        _diff_in_kernel,
        out_shape=(jax.ShapeDtypeStruct((T, D), BF16), jax.ShapeDtypeStruct((D, T), BF16),
                   jax.ShapeDtypeStruct((T, D), BF16), jax.ShapeDtypeStruct((T_P, D), F32),
                   jax.ShapeDtypeStruct((T_P, D), F32)),
        grid=(T // TMX,),
        in_specs=[_row_spec(D, TMX), _mod_spec(layer, TMX), _full((1, D)), _full((D, 3 * D)),
                  _full((1, LANE)), _full((1, LANE)), _rope_spec(), _rope_spec()],
        out_specs=(_row_spec(D, TMX), pl.BlockSpec((D, TMX), lambda i: (0, i)), _row_spec(D, TMX),
                   st_spec, st_spec),
        scratch_shapes=[pltpu.VMEM((TMX, D), F32)],
        compiler_params=_params(),
        name="diff_in",
    )(x, mods, g1, w, qn, kn, cos_t, sin_t)


def _sconv_kernel(x_ref, prev_ref, next_ref, mod_ref, g1_ref, win_ref, cw_ref, wout_ref, g2_ref,
                  y_ref, hf_ref, p_ref, mixed_ref):
    tm, hb = TMX, HALO_F32
    is_prompt = pl.program_id(0) < T_P // tm
    has_prev, has_next = _seq_edges(pl.program_id(0), tm)
    mods = _mod_parts(mod_ref)
    x = x_ref[...]
    mod_in = lambda a: _modulate(a, g1_ref[...], mods[0], mods[1])
    zeros = jnp.zeros((hb, D), F32)
    lhs = jnp.concatenate([jnp.where(has_prev, mod_in(prev_ref[...]), zeros), mod_in(x),
                           jnp.where(has_next, mod_in(next_ref[...]), zeros)],
                          axis=0).astype(BF16)
    for c in range(D // HC):
        cs = slice(c * HC, (c + 1) * HC)
        gb = _dot(lhs, win_ref[:, c * HC:(c + 1) * HC])
        gc = _dot(lhs, win_ref[:, D + c * HC:D + (c + 1) * HC])
        u = _dot(lhs, win_ref[:, 2 * D + c * HC:2 * D + (c + 1) * HC])
        p_ref[:, cs] = gc * u
        conv = _conv3(p_ref.at[:, cs], cw_ref[:, cs], hb, tm, is_prompt)
        mixed_ref[:, cs] = (gb[hb:hb + tm] * conv).astype(BF16)
    y, hf = _residual_and_ffn_in(x, _dot(mixed_ref[...], wout_ref[...]), mods, g2_ref[...])
    y_ref[...] = y
    hf_ref[...] = hf


def _sconv(x, mods, layer, g1, win, cw, wout, g2):
    tm, hb = TMX, HALO_F32
    r = tm // hb
    once = dict(pipeline_mode=pl.Buffered(1))
    return pl.pallas_call(
        _sconv_kernel,
        out_shape=(jax.ShapeDtypeStruct((T, D), F32), jax.ShapeDtypeStruct((T, D), BF16)),
        grid=(T // tm,),
        in_specs=[_row_spec(D, tm),
                  pl.BlockSpec((hb, D), lambda i: (jnp.maximum(i * r - 1, 0), 0)),
                  pl.BlockSpec((hb, D), lambda i: (jnp.minimum((i + 1) * r, T // hb - 1), 0)),
                  _mod_spec(layer, tm), _full((1, D)),
                  pl.BlockSpec((D, 3 * D), lambda i: (0, 0), **once), _full((3, D)),
                  pl.BlockSpec((D, D), lambda i: (0, 0), **once), _full((1, D))],
        out_specs=(_row_spec(D, tm), _row_spec(D, tm)),
        scratch_shapes=[pltpu.VMEM((tm + 2 * hb, D), F32), pltpu.VMEM((tm, D), BF16)],
        compiler_params=_params(),
        name="sconv",
    )(x, x, x, mods, g1, win, cw, wout, g2)


def _gmlp_kernel(x_ref, mod_ref, g1_ref, win_ref, vn_ref, ws_ref, bs_ref, wout_ref, g2_ref,
                 y_ref, hf_ref, gated_ref):
    mods = _mod_parts(mod_ref)
    x = x_ref[...]
    h = _modulate(x, g1_ref[...], mods[0], mods[1]).astype(BF16)
    z = _dot(h, win_ref[...])
    z = z * (0.5 * (1.0 + jnp.tanh(math.sqrt(2.0 / math.pi) * (z + 0.044715 * (z * z * z)))))
    u = z[:, :D]
    v = _rms(z[:, D:], vn_ref[...]).astype(BF16)
    bs = bs_ref[...]
    for g in range(GMLP_GROUPS):
        cs = slice(g * LANE, (g + 1) * LANE)
        w = ws_ref[g]
        for r in range(TMX // GMLP_CHUNK):
            rs = slice(r * GMLP_CHUNK, (r + 1) * GMLP_CHUNK)
            mixed = _dot(w, v[rs, cs]) + bs[:, g:g + 1]
            gated_ref[rs, cs] = (u[rs, cs] * mixed).astype(BF16)
    y, hf = _residual_and_ffn_in(x, _dot(gated_ref[...], wout_ref[...]), mods, g2_ref[...])
    y_ref[...] = y
    hf_ref[...] = hf


def _gmlp(x, mods, layer, g1, win, vn, ws, bs_t, wout, g2):
    return pl.pallas_call(
        _gmlp_kernel,
        out_shape=(jax.ShapeDtypeStruct((T, D), F32), jax.ShapeDtypeStruct((T, D), BF16)),
        grid=(T // TMX,),
        in_specs=[_row_spec(D, TMX), _mod_spec(layer, TMX), _full((1, D)), _full((D, 2 * D)), _full((1, D)),
                  _full((GMLP_GROUPS, GMLP_CHUNK, GMLP_CHUNK)), _full((GMLP_CHUNK, GMLP_GROUPS)),
                  _full((D, D)), _full((1, D))],
        out_specs=(_row_spec(D, TMX), _row_spec(D, TMX)),
        scratch_shapes=[pltpu.VMEM((TMX, D), BF16)],
        compiler_params=_params(),
        name="gmlp",
    )(x, mods, g1, win, vn, ws, bs_t, wout, g2)


def _rope_tables():
    rows = DEC_SEQ // GRID_W
    row = jnp.repeat(jnp.arange(rows, dtype=F32), GRID_W)
    col = jnp.tile(jnp.arange(GRID_W, dtype=F32), rows)
    n_freq = MLA_ROPE // 4
    inv_freq = ROPE_THETA ** (-jnp.arange(n_freq, dtype=F32) / n_freq)
    ang = jnp.concatenate([row[:, None] * inv_freq, col[:, None] * inv_freq], axis=-1)
    cos, sin = jnp.cos(ang), jnp.sin(ang)
    cos = jnp.concatenate([cos, jnp.ones((TMX, cos.shape[1]), F32)], axis=0)
    sin = jnp.concatenate([sin, jnp.zeros((TMX, sin.shape[1]), F32)], axis=0)
    return jnp.tile(cos, (1, 4)), jnp.tile(jnp.concatenate([-sin, sin], axis=1), (1, 2))


def _pad_lanes(a, width=LANE):
    return jnp.pad(a, [(0, 0)] * (a.ndim - 1) + [(0, width - a.shape[-1])])


def _heads_first(a):
    return jnp.swapaxes(a, -3, -2).reshape(a.shape[:-3] + (D,))


def kernel(x_prompt, x_sample, cache_mla_ckv, cache_mla_krope, cache_diff_k, cache_diff_v, c, c_ctx, ada_w, ada_b, norm1_g, norm2_g, mla_w_down, mla_q_norm, mla_kv_norm, mla_w_uq, mla_w_uk, mla_w_uv, mla_qn_nope, mla_qn_rope, mla_kn_nope, mla_kn_rope, mla_w_o, diff_w_qkv, diff_qn, diff_kn, diff_lq1, diff_lk1, diff_lq2, diff_lk2, diff_head_norm, diff_w_o, sconv_w_in, sconv_w, sconv_w_out, gmlp_w_in, gmlp_v_norm, gmlp_w_s, gmlp_b_s, gmlp_w_out, ffn_w_in, ffn_conv_w, ffn_conv_b, ffn_w_out):
    xp, xs = x_prompt.reshape(T_P, D), x_sample.reshape(T_S, D)
    cond = jnp.concatenate([c_ctx[None], c, jnp.zeros((MOD_ROWS - 1 - DEC_BATCH, D), F32)], axis=0)
    mods = _adaln(cond, ada_w, ada_b).reshape(DEPTH * MOD_ROWS, 1, N_MOD * D)
    cos_t, sin_t = _rope_tables()
    row = lambda a: a.reshape(1, -1)

    ffn_w = (ffn_w_in.astype(BF16), ffn_conv_w, ffn_conv_b[:, None, :], ffn_w_out.astype(BF16))

    def ffn(y, hf, i):
        return _ffn(y, hf, mods, i, *ffn_w, split_out=(i == DEPTH - 1))

    wd = _pad_lanes(mla_w_down[0], MLA_DOWN_W).astype(BF16)
    wuq = _pad_lanes(mla_w_uq[0].reshape(MLA_Q_LORA, MLA_HEADS, MLA_NOPE + MLA_ROPE), MLA_DK)
    wuq = wuq.reshape(MLA_Q_LORA, MLA_HEADS * MLA_DK).astype(BF16)
    q, ckv, kr = _mla_in(xp, xs, mods, 0, row(norm1_g[0]), wd, row(mla_q_norm[0]), row(mla_kv_norm[0]),
                         _pad_lanes(row(mla_kn_rope[0])), wuq, row(mla_qn_nope[0]),
                         _pad_lanes(row(mla_qn_rope[0])), cos_t, sin_t)
    state_ckv = ckv[:T_P].reshape(BATCH, 1, SEQ, MLA_KV_LORA)
    state_kr = kr[:T_P, :MLA_ROPE].reshape(BATCH, 1, SEQ, MLA_ROPE)
    kv_w = (mla_w_uk[0].astype(BF16), mla_w_uv[0].astype(BF16), row(mla_kn_nope[0]))
    kt, v = _mla_kv(ckv, kr, *kv_w)
    kt_cache, v_cache = _mla_kv(cache_mla_ckv[:, 0].reshape(DEC_BATCH * PAST, MLA_KV_LORA),
                                _pad_lanes(cache_mla_krope[:, 0]).reshape(DEC_BATCH * PAST, LANE), *kv_w)
    y, hf = _attention(_mla_heads, "mla_attn", (xp, xs), q, kt, v, kt_cache, v_cache, [], mods, 0,
                       mla_w_o[0].astype(BF16), row(norm2_g[0]), MLA_HEADS * MLA_DK, MLA_HEADS * MLA_V)
    x, = ffn(y, hf, 0)

    lam_init = 0.8 - 0.6 * math.exp(-0.3 * 1)
    wq, wk, wv = jnp.split(diff_w_qkv[0], 3, axis=-1)
    perm = lambda w: _heads_first(w.reshape(D, 2, DIFF_HEADS, DIFF_HD))
    w_qkv = jnp.concatenate([perm(wq), perm(wk), wv], axis=-1).astype(BF16)
    pair = lambda g: row(jnp.tile(g, 2))
    q, kt, v, kst, vst = _diff_in(x, mods, 1, row(norm1_g[1]), w_qkv, pair(diff_qn[0]), pair(diff_kn[0]),
                                 cos_t, sin_t)
    state_dk = jnp.swapaxes(kst.reshape(BATCH, SEQ, DIFF_HEADS, 2, DIFF_HD), 2, 3)[:, None]
    state_dv = vst.reshape(BATCH, 1, SEQ, DIFF_HEADS, 2 * DIFF_HD)
    kt_cache = _heads_first(cache_diff_k[:, 0]).astype(BF16).reshape(DEC_BATCH * PAST, D).T
    v_cache = cache_diff_v[:, 0].reshape(DEC_BATCH * PAST, D).astype(BF16)
    smalls = [row(diff_lq1[0]), row(diff_lk1[0]), row(diff_lq2[0]), row(diff_lk2[0]),
              row(diff_head_norm[0])]
    y, hf = _attention(functools.partial(_diff_heads, lam_init), "diff_attn", (x,), q, kt, v, kt_cache, v_cache,
                       smalls, mods, 1, diff_w_o[0].astype(BF16), row(norm2_g[1]), D, D)
    x, = ffn(y, hf, 1)

    y, hf = _sconv(x, mods, 2, row(norm1_g[2]), sconv_w_in[0].astype(BF16), sconv_w[0],
                   sconv_w_out[0].astype(BF16), row(norm2_g[2]))
    x, = ffn(y, hf, 2)

    y, hf = _gmlp(x, mods, 3, row(norm1_g[3]), gmlp_w_in[0].astype(BF16), row(gmlp_v_norm[0]),
                  gmlp_w_s[0].astype(BF16), gmlp_b_s[0].T, gmlp_w_out[0].astype(BF16), row(norm2_g[3]))
    yp, ys = ffn(y, hf, 3)

    return (yp.reshape(BATCH, SEQ, D), ys.reshape(DEC_BATCH, DEC_SEQ, D),
            state_ckv, state_kr, state_dk, state_dv)
```

```python
import functools
import math

import jax
import jax.numpy as jnp
from jax import lax
from jax.experimental import pallas as pl
from jax.experimental.pallas import tpu as pltpu

D = 1024
BATCH, SEQ = 16, 256
DEC_BATCH, DEC_SEQ = 8, 2048
PAST = 256
DEPTH = 4
GRID_W = 64
EPS = 1e-6
ROPE_THETA = 10000.0
N_MOD = 6
MLA_HEADS, MLA_NOPE, MLA_ROPE, MLA_V = 8, 128, 64, 128
MLA_Q_LORA, MLA_KV_LORA = 768, 256
DIFF_HEADS, DIFF_HD = 8, 64
GMLP_CHUNK, GMLP_GROUPS = 128, 8
FFN_HIDDEN = 2816

T_P = BATCH * SEQ
T_S = DEC_BATCH * DEC_SEQ
T = T_P + T_S
TM = 256
TMX = 512
FFN_TM = 512
NB_P = T_P // TM
NB = T // TM
BPS = DEC_SEQ // TM
MOD_ROWS = 16
LANE = 128
MLA_DK = 256
HC = 256
HALO_BF16 = 16
HALO_F32 = 8
VMEM_LIMIT = 56 * 1024 * 1024
LOG2E = math.log2(math.e)
MLA_Q_SCALE = (MLA_NOPE + MLA_ROPE) ** -0.5 * LOG2E
DIFF_Q_SCALE = DIFF_HD ** -0.5 * LOG2E

F32 = jnp.float32
BF16 = jnp.bfloat16


def _params(n_axes=1, vmem=VMEM_LIMIT):
    return pltpu.CompilerParams(dimension_semantics=("arbitrary",) * n_axes,
                                vmem_limit_bytes=vmem)


def _dot(a, b):
    return jnp.dot(a, b, preferred_element_type=F32)


def _rms(x, g):
    ms = jnp.mean(x * x, axis=-1, keepdims=True)
    return (x * lax.rsqrt(ms + EPS)) * g


def _modulate(x, g, shift, scale):
    return _rms(x, g) * (1.0 + scale) + shift


def _sigmoid(x):
    return 1.0 / (1.0 + jnp.exp(-x))


def _mod_parts(mod_ref):
    m = mod_ref[0]
    return [m[:, k * D:(k + 1) * D] for k in range(N_MOD)]


def _full(shape):
    return pl.BlockSpec(shape, lambda *_: (0,) * len(shape))


def _row_spec(width, rows=TM):
    return pl.BlockSpec((rows, width), lambda i: (i, 0))


def _split_specs(width, tm=TM):
    nbp = T_P // tm
    return [pl.BlockSpec((tm, width), lambda i: (jnp.minimum(i, nbp - 1), 0)),
            pl.BlockSpec((tm, width), lambda i: (jnp.maximum(i - nbp, 0), 0))]


def _split_rows(p_ref, s_ref, tm=TM):
    return jnp.where(pl.program_id(0) < T_P // tm, p_ref[...], s_ref[...])


def _mod_spec(layer, tm=TM):
    nbp, bps = T_P // tm, DEC_SEQ // tm

    def imap(i):
        j = jnp.maximum(i - nbp, 0)
        return (layer * MOD_ROWS + jnp.where(i < nbp, 0, 1 + j // bps), 0, 0)
    return pl.BlockSpec((1, 1, N_MOD * D), imap)


def _rope_spec(tm=TMX):
    nbp, bps = T_P // tm, DEC_SEQ // tm

    def imap(i):
        j = jnp.maximum(i - nbp, 0)
        return (jnp.where(i < nbp, bps, j % bps), 0)
    return pl.BlockSpec((tm, LANE), imap)


def _seq_edges(i, tm=TM):
    row0 = i * tm
    seq_len = jnp.where(row0 < T_P, SEQ, DEC_SEQ)
    return row0 % seq_len != 0, (row0 + tm) % seq_len != 0


def _group_rms(xb, gain, group_ones):
    sq = xb * xb
    hi = sq.astype(BF16)
    lo = (sq - hi.astype(F32)).astype(BF16)
    ms = (_dot(hi, group_ones) + _dot(lo, group_ones)) * (1.0 / 64.0)
    return (xb * lax.rsqrt(ms + EPS)) * gain


def _rope128(xb, cos, sin, first):
    swapped = jnp.where(first, pltpu.roll(xb, LANE - 32, 1), pltpu.roll(xb, 32, 1))
    return xb * cos + swapped * sin


ADA_TN = 1536


def _adaln_kernel(cond_ref, w_ref, b_ref, o_ref):
    c = cond_ref[...]
    a = (c * _sigmoid(c)).astype(BF16)
    o_ref[0] = _dot(a, w_ref[0].astype(BF16)) + b_ref[0]


def _adaln(cond, ada_w, ada_b):
    n = N_MOD * D
    return pl.pallas_call(
        _adaln_kernel,
        out_shape=jax.ShapeDtypeStruct((DEPTH, MOD_ROWS, n), F32),
        grid=(DEPTH, n // ADA_TN),
        in_specs=[pl.BlockSpec((MOD_ROWS, D), lambda l, j: (0, 0)),
                  pl.BlockSpec((1, D, ADA_TN), lambda l, j: (l, 0, j)),
                  pl.BlockSpec((1, 1, ADA_TN), lambda l, j: (l, 0, j))],
        out_specs=pl.BlockSpec((1, MOD_ROWS, ADA_TN), lambda l, j: (l, 0, j)),
        compiler_params=_params(2),
        name="adaln",
    )(cond, ada_w, ada_b.reshape(DEPTH, 1, n))


MLA_DOWN_W = MLA_Q_LORA + MLA_KV_LORA + LANE


def _mla_in_kernel(xp_ref, xs_ref, mod_ref, g1_ref, wd_ref, qn_ref, kvn_ref, knr_ref, wuq_ref,
                   qnn_ref, qnr_ref, cos_ref, sin_ref, q_ref, ckv_ref, kr_ref):
    shift, scale = _mod_parts(mod_ref)[:2]
    h = _modulate(_split_rows(xp_ref, xs_ref), g1_ref[...], shift, scale).astype(BF16)
    d = _dot(h, wd_ref[...])
    cq = _rms(d[:, :MLA_Q_LORA], qn_ref[...]).astype(BF16)
    ckv_ref[...] = _rms(d[:, MLA_Q_LORA:MLA_Q_LORA + MLA_KV_LORA], kvn_ref[...])

    lane = lax.broadcasted_iota(jnp.int32, (1, LANE), 1)
    first = (lane % 64) < 32
    cos, sin = cos_ref[...], sin_ref[...]

    def rope_part(xb, gain):
        ms = jnp.sum(xb * xb, axis=-1, keepdims=True) * (1.0 / MLA_ROPE)
        return _rope128((xb * lax.rsqrt(ms + EPS)) * gain, cos, sin, first)

    kr_ref[...] = rope_part(d[:, MLA_Q_LORA + MLA_KV_LORA:], knr_ref[...])

    q = _dot(cq, wuq_ref[...])
    for hh in range(MLA_HEADS):
        c0 = hh * MLA_DK
        qn = _rms(q[:, c0:c0 + LANE], qnn_ref[...])
        qr = rope_part(q[:, c0 + LANE:c0 + MLA_DK], qnr_ref[...])
        q_ref[:, c0:c0 + LANE] = (qn * MLA_Q_SCALE).astype(BF16)
        q_ref[:, c0 + LANE:c0 + MLA_DK] = (qr * MLA_Q_SCALE).astype(BF16)


def _mla_in(xp, xs, mods, layer, g1, wd, qn, kvn, knr, wuq, qnn, qnr, cos_t, sin_t):
    return pl.pallas_call(
        _mla_in_kernel,
        out_shape=(jax.ShapeDtypeStruct((T, MLA_HEADS * MLA_DK), BF16),
                   jax.ShapeDtypeStruct((T, MLA_KV_LORA), F32),
                   jax.ShapeDtypeStruct((T, LANE), F32)),
        grid=(NB,),
        in_specs=[*_split_specs(D), _mod_spec(layer), _full((1, D)), _full((D, MLA_DOWN_W)),
                  _full((1, MLA_Q_LORA)), _full((1, MLA_KV_LORA)), _full((1, LANE)),
                  _full((MLA_Q_LORA, MLA_HEADS * MLA_DK)), _full((1, LANE)), _full((1, LANE)),
                  _rope_spec(TM), _rope_spec(TM)],
        out_specs=(_row_spec(MLA_HEADS * MLA_DK), _row_spec(MLA_KV_LORA), _row_spec(LANE)),
        compiler_params=_params(),
        name="mla_in",
    )(xp, xs, mods, g1, wd, qn, kvn, knr, wuq, qnn, qnr, cos_t, sin_t)


def _mla_kv_kernel(ckv_ref, kr_ref, wuk_ref, wuv_ref, knn_ref, kt_ref, v_ref):
    c = ckv_ref[...].astype(BF16)
    kn = _dot(c, wuk_ref[...])
    v_ref[...] = _dot(c, wuv_ref[...]).astype(BF16)
    kr_t = kr_ref[...].T.astype(BF16)
    for hh in range(MLA_HEADS):
        r0 = hh * MLA_DK
        kh = _rms(kn[:, hh * LANE:(hh + 1) * LANE], knn_ref[...])
        kt_ref[r0:r0 + LANE, :] = kh.T.astype(BF16)
        kt_ref[r0 + LANE:r0 + MLA_DK, :] = kr_t


def _mla_kv(ckv, kr, wuk, wuv, knn):
    n = ckv.shape[0]
    return pl.pallas_call(
        _mla_kv_kernel,
        out_shape=(jax.ShapeDtypeStruct((MLA_HEADS * MLA_DK, n), BF16),
                   jax.ShapeDtypeStruct((n, MLA_HEADS * MLA_V), BF16)),
        grid=(n // TMX,),
        in_specs=[_row_spec(MLA_KV_LORA, TMX), _row_spec(LANE, TMX),
                  _full((MLA_KV_LORA, MLA_HEADS * MLA_NOPE)), _full((MLA_KV_LORA, MLA_HEADS * MLA_V)),
                  _full((1, LANE))],
        out_specs=(pl.BlockSpec((MLA_HEADS * MLA_DK, TMX), lambda i: (0, i)),
                   _row_spec(MLA_HEADS * MLA_V, TMX)),
        compiler_params=_params(),
        name="mla_kv",
    )(ckv, kr, wuk, wuv, knn)


def _scores(q, kts, rows):
    parts = [_dot(q, kt_ref[rows, :]) for kt_ref in kts]
    return parts[0] if len(parts) == 1 else jnp.concatenate(parts, axis=1)


def _softmax_pv(s2, vs, cols):
    p = jnp.exp2(s2 - jnp.max(s2, axis=-1, keepdims=True)).astype(BF16)
    o, k0 = None, 0
    for v_ref in vs:
        v = v_ref[:, cols]
        part = _dot(p[:, k0:k0 + v.shape[0]], jnp.concatenate([v, jnp.ones_like(v)], axis=1))
        o = part if o is None else o + part
        k0 += v.shape[0]
    return o[:, :LANE] * (1.0 / o[:, LANE:LANE + 1])


def _mla_heads(q_ref, kts, vs, o_ref):
    for hh in range(MLA_HEADS):
        ks = slice(hh * MLA_DK, (hh + 1) * MLA_DK)
        cs = slice(hh * MLA_V, (hh + 1) * MLA_V)
        o_ref[:, cs] = _softmax_pv(_scores(q_ref[:, ks], kts, ks), vs, cs).astype(BF16)


def _diff_heads(lam_init, lq1_ref, lk1_ref, lq2_ref, lk2_ref, hn_ref, q_ref, kts, vs, o_ref):
    lam = (jnp.exp(jnp.sum(lq1_ref[...] * lk1_ref[...], axis=-1, keepdims=True))
           - jnp.exp(jnp.sum(lq2_ref[...] * lk2_ref[...], axis=-1, keepdims=True)) + lam_init)
    lane = lax.broadcasted_iota(jnp.int32, (1, LANE), 1)
    left = lane < DIFF_HD
    zero = jnp.zeros((), BF16)
    for hh in range(DIFF_HEADS):
        cs = slice(hh * LANE, (hh + 1) * LANE)
        qb = q_ref[:, cs]
        o0 = _softmax_pv(_scores(jnp.where(left, qb, zero), kts, cs), vs, cs)
        o1 = _softmax_pv(_scores(jnp.where(left, zero, qb), kts, cs), vs, cs)
        o = o0 - lam * o1
        o_ref[:, cs] = (_rms(o, hn_ref[...]) * (1.0 - lam_init)).astype(BF16)


def _residual_and_ffn_in(x, mix, mods, g2):
    _, _, gate1, shift2, scale2, _ = mods
    y = x + gate1 * mix
    return y, _modulate(y, g2, shift2, scale2).astype(BF16)


def _attn_kernel(heads, n_x, n_small, *refs):
    x_refs, refs = refs[:n_x], refs[n_x:]
    q_ref, ktp_ref, ktn_ref, ktc_ref, vp_ref, vn_ref, vc_ref = refs[:7]
    smalls, (mod_ref, wo_ref, g2_ref, y_ref, hf_ref, o_ref) = refs[7:7 + n_small], refs[7 + n_small:]
    is_prompt = pl.program_id(0) < NB_P

    @pl.when(is_prompt)
    def _():
        heads(*smalls, q_ref, [ktp_ref], [vp_ref], o_ref)

    @pl.when(jnp.logical_not(is_prompt))
    def _():
        heads(*smalls, q_ref, [ktc_ref, ktn_ref], [vc_ref, vn_ref], o_ref)

    x = x_refs[0][...] if n_x == 1 else _split_rows(*x_refs)
    y, hf = _residual_and_ffn_in(x, _dot(o_ref[...], wo_ref[...]), _mod_parts(mod_ref), g2_ref[...])
    y_ref[...] = y
    hf_ref[...] = hf


def _attention(heads, name, xs, q, kt, v, kt_cache, v_cache, smalls, mods, layer, wo, g2, dk, dv):
    assert SEQ == TM and T_P % DEC_SEQ == 0
    pro = lambda i: jnp.minimum(i, NB_P - 1)
    lat = lambda i: jnp.maximum(i - NB_P, 0) // BPS
    lat_new = lambda i: T_P // DEC_SEQ + lat(i)
    x_specs = [_row_spec(D)] if len(xs) == 1 else _split_specs(D)
    return pl.pallas_call(
        functools.partial(_attn_kernel, heads, len(xs), len(smalls)),
        out_shape=(jax.ShapeDtypeStruct((T, D), F32), jax.ShapeDtypeStruct((T, D), BF16)),
        grid=(NB,),
        in_specs=[*x_specs, _row_spec(dk),
                  pl.BlockSpec((dk, SEQ), lambda i: (0, pro(i))),
                  pl.BlockSpec((dk, DEC_SEQ), lambda i: (0, lat_new(i))),
                  pl.BlockSpec((dk, PAST), lambda i: (0, lat(i))),
                  pl.BlockSpec((SEQ, dv), lambda i: (pro(i), 0)),
                  pl.BlockSpec((DEC_SEQ, dv), lambda i: (lat_new(i), 0)),
                  pl.BlockSpec((PAST, dv), lambda i: (lat(i), 0)),
                  *[_full(s.shape) for s in smalls], _mod_spec(layer), _full((dv, D)), _full((1, D))],
        out_specs=(_row_spec(D), _row_spec(D)),
        scratch_shapes=[pltpu.VMEM((TM, dv), BF16)],
        compiler_params=_params(),
        name=name,
    )(*xs, q, kt, kt, kt_cache, v, v, v_cache, *smalls, mods, wo, g2)


def _zero_row(a, r, cond):
    r0 = r // HALO_F32 * HALO_F32
    tile = a[r0:r0 + HALO_F32]
    row = lax.broadcasted_iota(jnp.int32, (HALO_F32, 1), 0)
    tile = jnp.where(row == r - r0, jnp.where(cond, 0.0, tile), tile)
    return jnp.concatenate([a[:r0], tile, a[r0 + HALO_F32:]], axis=0)


def _conv3(zs_ref, w, off, tm=TM, is_prompt=False):
    z = zs_ref[...]
    n = z.shape[0]
    z_prev = pltpu.roll(z, 1, 0)[off:off + tm]
    z_next = pltpu.roll(z, n - 1, 0)[off:off + tm]
    for start in range(SEQ, tm, SEQ):
        z_prev = _zero_row(z_prev, start, is_prompt)
        z_next = _zero_row(z_next, start - 1, is_prompt)
    return z_prev * w[0:1] + z[off:off + tm] * w[1:2] + z_next * w[2:3]


def _ffn_kernel(n_out, y_ref, hf_ref, prev_ref, next_ref, mod_ref, win_ref, cw_ref, cb_ref, wout_ref,
                *rest):
    out_refs, (lhs_ref, z_ref, act_ref) = rest[:n_out], rest[n_out:]
    tm, hb = FFN_TM, HALO_BF16
    is_prompt = pl.program_id(0) < T_P // tm
    has_prev, has_next = _seq_edges(pl.program_id(0), tm)
    zeros = jnp.zeros((hb, D), BF16)
    lhs_ref[0:hb, :] = jnp.where(has_prev, prev_ref[...], zeros)
    lhs_ref[hb:hb + tm, :] = hf_ref[...]
    lhs_ref[hb + tm:, :] = jnp.where(has_next, next_ref[...], zeros)
    lhs = lhs_ref[...]
    for c in range(FFN_HIDDEN // HC):
        gs = slice(c * HC, (c + 1) * HC)
        us = slice(FFN_HIDDEN + c * HC, FFN_HIDDEN + (c + 1) * HC)
        z_ref[:, gs] = _dot(lhs, win_ref[0, :, gs])
        z_ref[:, us] = _dot(lhs, win_ref[0, :, us])
        g = _conv3(z_ref.at[:, gs], cw_ref[0, :, gs], hb, tm, is_prompt) + cb_ref[0, :, gs]
        u = _conv3(z_ref.at[:, us], cw_ref[0, :, us], hb, tm, is_prompt) + cb_ref[0, :, us]
        act_ref[:, gs] = ((g * _sigmoid(g)) * u).astype(BF16)
    gate2 = _mod_parts(mod_ref)[5]
    out = y_ref[...] + gate2 * _dot(act_ref[...], wout_ref[0])
    if len(out_refs) == 1:
        out_refs[0][...] = out
    else:
        @pl.when(is_prompt)
        def _():
            out_refs[0][...] = out

        @pl.when(jnp.logical_not(is_prompt))
        def _():
            out_refs[1][...] = out


def _ffn(y, hf, mods, layer, win, cw, cb, wout, split_out=False):
    tm, hb = FFN_TM, HALO_BF16
    r = tm // hb
    if split_out:
        out_shape = (jax.ShapeDtypeStruct((T_P, D), F32), jax.ShapeDtypeStruct((T_S, D), F32))
        out_specs = tuple(_split_specs(D, tm))
    else:
        out_shape, out_specs = (jax.ShapeDtypeStruct((T, D), F32),), (_row_spec(D, tm),)
    w_specs = [pl.BlockSpec((1,) + w.shape[1:], lambda i: (layer, 0, 0), pipeline_mode=pl.Buffered(1))
               for w in (win, cw, cb, wout)]
    return pl.pallas_call(
        functools.partial(_ffn_kernel, len(out_shape)),
        out_shape=out_shape,
        grid=(T // tm,),
        in_specs=[_row_spec(D, tm), _row_spec(D, tm),
                  pl.BlockSpec((hb, D), lambda i: (jnp.maximum(i * r - 1, 0), 0)),
                  pl.BlockSpec((hb, D), lambda i: (jnp.minimum((i + 1) * r, T // hb - 1), 0)),
                  _mod_spec(layer, tm), *w_specs],
        out_specs=out_specs,
        scratch_shapes=[pltpu.VMEM((tm + 2 * hb, D), BF16),
                        pltpu.VMEM((tm + 2 * hb, 2 * FFN_HIDDEN), F32),
                        pltpu.VMEM((tm, FFN_HIDDEN), BF16)],
        compiler_params=_params(),
        name="ffn",
    )(y, hf, hf, hf, mods, win, cw, cb, wout)


def _diff_in_kernel(x_ref, mod_ref, g1_ref, w_ref, qn_ref, kn_ref, cos_ref, sin_ref,
                    q_ref, kt_ref, v_ref, kst_ref, vst_ref, kn_ref_scratch):
    shift, scale = _mod_parts(mod_ref)[:2]
    h = _modulate(x_ref[...], g1_ref[...], shift, scale).astype(BF16)
    z = _dot(h, w_ref[...])
    lane = lax.broadcasted_iota(jnp.int32, (1, LANE), 1)
    first = (lane % 64) < 32
    same_half = (lax.broadcasted_iota(jnp.int32, (LANE, LANE), 0) < DIFF_HD) == (
        lax.broadcasted_iota(jnp.int32, (LANE, LANE), 1) < DIFF_HD)
    group_ones = jnp.where(same_half, 1.0, 0.0).astype(BF16)
    cos, sin = cos_ref[...], sin_ref[...]
    v = z[:, 2 * D:]
    v_ref[...] = v.astype(BF16)
    for hh in range(DIFF_HEADS):
        cs = slice(hh * LANE, (hh + 1) * LANE)
        qn = _group_rms(z[:, hh * LANE:(hh + 1) * LANE], qn_ref[...], group_ones)
        q_ref[:, cs] = (_rope128(qn, cos, sin, first) * DIFF_Q_SCALE).astype(BF16)
        kn = _group_rms(z[:, D + hh * LANE:D + (hh + 1) * LANE], kn_ref[...], group_ones)
        kt_ref[hh * LANE:(hh + 1) * LANE, :] = _rope128(kn, cos, sin, first).T.astype(BF16)
        kn_ref_scratch[:, cs] = kn

    @pl.when(pl.program_id(0) < T_P // TMX)
    def _():
        vst_ref[...] = v
        kst_ref[...] = kn_ref_scratch[...]


def _diff_in(x, mods, layer, g1, w, qn, kn, cos_t, sin_t):
    st_spec = _split_specs(D, TMX)[0]
    return pl.pallas_call(
        _diff_in_kernel,
        out_shape=(jax.ShapeDtypeStruct((T, D), BF16), jax.ShapeDtypeStruct((D, T), BF16),
                   jax.ShapeDtypeStruct((T, D), BF16), jax.ShapeDtypeStruct((T_P, D), F32),
                   jax.ShapeDtypeStruct((T_P, D), F32)),
        grid=(T // TMX,),
        in_specs=[_row_spec(D, TMX), _mod_spec(layer, TMX), _full((1, D)), _full((D, 3 * D)),
                  _full((1, LANE)), _full((1, LANE)), _rope_spec(), _rope_spec()],
        out_specs=(_row_spec(D, TMX), pl.BlockSpec((D, TMX), lambda i: (0, i)), _row_spec(D, TMX),
                   st_spec, st_spec),
        scratch_shapes=[pltpu.VMEM((TMX, D), F32)],
        compiler_params=_params(),
        name="diff_in",
    )(x, mods, g1, w, qn, kn, cos_t, sin_t)


def _sconv_kernel(x_ref, prev_ref, next_ref, mod_ref, g1_ref, win_ref, cw_ref, wout_ref, g2_ref,
                  y_ref, hf_ref, p_ref, mixed_ref):
    tm, hb = TMX, HALO_F32
    is_prompt = pl.program_id(0) < T_P // tm
    has_prev, has_next = _seq_edges(pl.program_id(0), tm)
    mods = _mod_parts(mod_ref)
    x = x_ref[...]
    mod_in = lambda a: _modulate(a, g1_ref[...], mods[0], mods[1])
    zeros = jnp.zeros((hb, D), F32)
    lhs = jnp.concatenate([jnp.where(has_prev, mod_in(prev_ref[...]), zeros), mod_in(x),
                           jnp.where(has_next, mod_in(next_ref[...]), zeros)],
                          axis=0).astype(BF16)
    for c in range(D // HC):
        cs = slice(c * HC, (c + 1) * HC)
        gb = _dot(lhs, win_ref[:, c * HC:(c + 1) * HC])
        gc = _dot(lhs, win_ref[:, D + c * HC:D + (c + 1) * HC])
        u = _dot(lhs, win_ref[:, 2 * D + c * HC:2 * D + (c + 1) * HC])
        p_ref[:, cs] = gc * u
        conv = _conv3(p_ref.at[:, cs], cw_ref[:, cs], hb, tm, is_prompt)
        mixed_ref[:, cs] = (gb[hb:hb + tm] * conv).astype(BF16)
    y, hf = _residual_and_ffn_in(x, _dot(mixed_ref[...], wout_ref[...]), mods, g2_ref[...])
    y_ref[...] = y
    hf_ref[...] = hf


def _sconv(x, mods, layer, g1, win, cw, wout, g2):
    tm, hb = TMX, HALO_F32
    r = tm // hb
    once = dict(pipeline_mode=pl.Buffered(1))
    return pl.pallas_call(
        _sconv_kernel,
        out_shape=(jax.ShapeDtypeStruct((T, D), F32), jax.ShapeDtypeStruct((T, D), BF16)),
        grid=(T // tm,),
        in_specs=[_row_spec(D, tm),
                  pl.BlockSpec((hb, D), lambda i: (jnp.maximum(i * r - 1, 0), 0)),
                  pl.BlockSpec((hb, D), lambda i: (jnp.minimum((i + 1) * r, T // hb - 1), 0)),
                  _mod_spec(layer, tm), _full((1, D)),
                  pl.BlockSpec((D, 3 * D), lambda i: (0, 0), **once), _full((3, D)),
                  pl.BlockSpec((D, D), lambda i: (0, 0), **once), _full((1, D))],
        out_specs=(_row_spec(D, tm), _row_spec(D, tm)),
        scratch_shapes=[pltpu.VMEM((tm + 2 * hb, D), F32), pltpu.VMEM((tm, D), BF16)],
        compiler_params=_params(),
        name="sconv",
    )(x, x, x, mods, g1, win, cw, wout, g2)


def _gmlp_kernel(x_ref, mod_ref, g1_ref, win_ref, vn_ref, ws_ref, bs_ref, wout_ref, g2_ref,
                 y_ref, hf_ref, gated_ref):
    mods = _mod_parts(mod_ref)
    x = x_ref[...]
    h = _modulate(x, g1_ref[...], mods[0], mods[1]).astype(BF16)
    z = _dot(h, win_ref[...])
    z = z * (0.5 * (1.0 + jnp.tanh(math.sqrt(2.0 / math.pi) * (z + 0.044715 * (z * z * z)))))
    u = z[:, :D]
    v = _rms(z[:, D:], vn_ref[...]).astype(BF16)
    bs = bs_ref[...]
    for g in range(GMLP_GROUPS):
        cs = slice(g * LANE, (g + 1) * LANE)
        w = ws_ref[g]
        for r in range(TMX // GMLP_CHUNK):
            rs = slice(r * GMLP_CHUNK, (r + 1) * GMLP_CHUNK)
            mixed = _dot(w, v[rs, cs]) + bs[:, g:g + 1]
            gated_ref[rs, cs] = (u[rs, cs] * mixed).astype(BF16)
    y, hf = _residual_and_ffn_in(x, _dot(gated_ref[...], wout_ref[...]), mods, g2_ref[...])
    y_ref[...] = y
    hf_ref[...] = hf


def _gmlp(x, mods, layer, g1, win, vn, ws, bs_t, wout, g2):
    return pl.pallas_call(
        _gmlp_kernel,
        out_shape=(jax.ShapeDtypeStruct((T, D), F32), jax.ShapeDtypeStruct((T, D), BF16)),
        grid=(T // TMX,),
        in_specs=[_row_spec(D, TMX), _mod_spec(layer, TMX), _full((1, D)), _full((D, 2 * D)), _full((1, D)),
                  _full((GMLP_GROUPS, GMLP_CHUNK, GMLP_CHUNK)), _full((GMLP_CHUNK, GMLP_GROUPS)),
                  _full((D, D)), _full((1, D))],
        out_specs=(_row_spec(D, TMX), _row_spec(D, TMX)),
        scratch_shapes=[pltpu.VMEM((TMX, D), BF16)],
        compiler_params=_params(),
        name="gmlp",
    )(x, mods, g1, win, vn, ws, bs_t, wout, g2)


def _rope_tables():
    rows = DEC_SEQ // GRID_W
    row = jnp.repeat(jnp.arange(rows, dtype=F32), GRID_W)
    col = jnp.tile(jnp.arange(GRID_W, dtype=F32), rows)
    n_freq = MLA_ROPE // 4
    inv_freq = ROPE_THETA ** (-jnp.arange(n_freq, dtype=F32) / n_freq)
    ang = jnp.concatenate([row[:, None] * inv_freq, col[:, None] * inv_freq], axis=-1)
    cos, sin = jnp.cos(ang), jnp.sin(ang)
    cos = jnp.concatenate([cos, jnp.ones((TMX, cos.shape[1]), F32)], axis=0)
    sin = jnp.concatenate([sin, jnp.zeros((TMX, sin.shape[1]), F32)], axis=0)
    return jnp.tile(cos, (1, 4)), jnp.tile(jnp.concatenate([-sin, sin], axis=1), (1, 2))


def _pad_lanes(a, width=LANE):
    return jnp.pad(a, [(0, 0)] * (a.ndim - 1) + [(0, width - a.shape[-1])])


def _heads_first(a):
    return jnp.swapaxes(a, -3, -2).reshape(a.shape[:-3] + (D,))


def kernel(x_prompt, x_sample, cache_mla_ckv, cache_mla_krope, cache_diff_k, cache_diff_v, c, c_ctx, ada_w, ada_b, norm1_g, norm2_g, mla_w_down, mla_q_norm, mla_kv_norm, mla_w_uq, mla_w_uk, mla_w_uv, mla_qn_nope, mla_qn_rope, mla_kn_nope, mla_kn_rope, mla_w_o, diff_w_qkv, diff_qn, diff_kn, diff_lq1, diff_lk1, diff_lq2, diff_lk2, diff_head_norm, diff_w_o, sconv_w_in, sconv_w, sconv_w_out, gmlp_w_in, gmlp_v_norm, gmlp_w_s, gmlp_b_s, gmlp_w_out, ffn_w_in, ffn_conv_w, ffn_conv_b, ffn_w_out):
    xp, xs = x_prompt.reshape(T_P, D), x_sample.reshape(T_S, D)
    cond = jnp.concatenate([c_ctx[None], c, jnp.zeros((MOD_ROWS - 1 - DEC_BATCH, D), F32)], axis=0)
    mods = _adaln(cond, ada_w, ada_b).reshape(DEPTH * MOD_ROWS, 1, N_MOD * D)
    cos_t, sin_t = _rope_tables()
    row = lambda a: a.reshape(1, -1)

    ffn_w = (ffn_w_in.astype(BF16), ffn_conv_w, ffn_conv_b[:, None, :], ffn_w_out.astype(BF16))

    def ffn(y, hf, i):
        return _ffn(y, hf, mods, i, *ffn_w, split_out=(i == DEPTH - 1))

    wd = _pad_lanes(mla_w_down[0], MLA_DOWN_W).astype(BF16)
    wuq = _pad_lanes(mla_w_uq[0].reshape(MLA_Q_LORA, MLA_HEADS, MLA_NOPE + MLA_ROPE), MLA_DK)
    wuq = wuq.reshape(MLA_Q_LORA, MLA_HEADS * MLA_DK).astype(BF16)
    q, ckv, kr = _mla_in(xp, xs, mods, 0, row(norm1_g[0]), wd, row(mla_q_norm[0]), row(mla_kv_norm[0]),
                         _pad_lanes(row(mla_kn_rope[0])), wuq, row(mla_qn_nope[0]),
                         _pad_lanes(row(mla_qn_rope[0])), cos_t, sin_t)
    state_ckv = ckv[:T_P].reshape(BATCH, 1, SEQ, MLA_KV_LORA)
    state_kr = kr[:T_P, :MLA_ROPE].reshape(BATCH, 1, SEQ, MLA_ROPE)
    kv_w = (mla_w_uk[0].astype(BF16), mla_w_uv[0].astype(BF16), row(mla_kn_nope[0]))
    kt, v = _mla_kv(ckv, kr, *kv_w)
    kt_cache, v_cache = _mla_kv(cache_mla_ckv[:, 0].reshape(DEC_BATCH * PAST, MLA_KV_LORA),
                                _pad_lanes(cache_mla_krope[:, 0]).reshape(DEC_BATCH * PAST, LANE), *kv_w)
    y, hf = _attention(_mla_heads, "mla_attn", (xp, xs), q, kt, v, kt_cache, v_cache, [], mods, 0,
                       mla_w_o[0].astype(BF16), row(norm2_g[0]), MLA_HEADS * MLA_DK, MLA_HEADS * MLA_V)
    x, = ffn(y, hf, 0)

    lam_init = 0.8 - 0.6 * math.exp(-0.3 * 1)
    wq, wk, wv = jnp.split(diff_w_qkv[0], 3, axis=-1)
    perm = lambda w: _heads_first(w.reshape(D, 2, DIFF_HEADS, DIFF_HD))
    w_qkv = jnp.concatenate([perm(wq), perm(wk), wv], axis=-1).astype(BF16)
    pair = lambda g: row(jnp.tile(g, 2))
    q, kt, v, kst, vst = _diff_in(x, mods, 1, row(norm1_g[1]), w_qkv, pair(diff_qn[0]), pair(diff_kn[0]),
                                 cos_t, sin_t)
    state_dk = jnp.swapaxes(kst.reshape(BATCH, SEQ, DIFF_HEADS, 2, DIFF_HD), 2, 3)[:, None]
    state_dv = vst.reshape(BATCH, 1, SEQ, DIFF_HEADS, 2 * DIFF_HD)
    kt_cache = _heads_first(cache_diff_k[:, 0]).astype(BF16).reshape(DEC_BATCH * PAST, D).T
    v_cache = cache_diff_v[:, 0].reshape(DEC_BATCH * PAST, D).astype(BF16)
    smalls = [row(diff_lq1[0]), row(diff_lk1[0]), row(diff_lq2[0]), row(diff_lk2[0]),
              row(diff_head_norm[0])]
    y, hf = _attention(functools.partial(_diff_heads, lam_init), "diff_attn", (x,), q, kt, v, kt_cache, v_cache,
                       smalls, mods, 1, diff_w_o[0].astype(BF16), row(norm2_g[1]), D, D)
    x, = ffn(y, hf, 1)

    y, hf = _sconv(x, mods, 2, row(norm1_g[2]), sconv_w_in[0].astype(BF16), sconv_w[0],
                   sconv_w_out[0].astype(BF16), row(norm2_g[2]))
    x, = ffn(y, hf, 2)

    y, hf = _gmlp(x, mods, 3, row(norm1_g[3]), gmlp_w_in[0].astype(BF16), row(gmlp_v_norm[0]),
                  gmlp_w_s[0].astype(BF16), gmlp_b_s[0].T, gmlp_w_out[0].astype(BF16), row(norm2_g[3]))
    yp, ys = ffn(y, hf, 3)

    return (yp.reshape(BATCH, SEQ, D), ys.reshape(DEC_BATCH, DEC_SEQ, D),
            state_ckv, state_kr, state_dk, state_dv)
```

```python
import functools
import math

import jax
import jax.numpy as jnp
from jax import lax
from jax.experimental import pallas as pl
from jax.experimental.pallas import tpu as pltpu

D = 1024
BATCH, SEQ = 16, 256
DEC_BATCH, DEC_SEQ = 8, 2048
PAST = 256
DEPTH = 4
GRID_W = 64
EPS = 1e-6
ROPE_THETA = 10000.0
N_MOD = 6
MLA_HEADS, MLA_NOPE, MLA_ROPE, MLA_V = 8, 128, 64, 128
MLA_Q_LORA, MLA_KV_LORA = 768, 256
DIFF_HEADS, DIFF_HD = 8, 64
GMLP_CHUNK, GMLP_GROUPS = 128, 8
FFN_HIDDEN = 2816

T_P = BATCH * SEQ
T_S = DEC_BATCH * DEC_SEQ
T = T_P + T_S
TM = 256
TMX = 512
TMG = 1024
FFN_TM = 512
NB_P = T_P // TM
NB = T // TM
BPS = DEC_SEQ // TM
MOD_ROWS = 16
LANE = 128
MLA_DK = 256
HC = 256
HALO_BF16 = 16
HALO_F32 = 8
VMEM_LIMIT = 56 * 1024 * 1024
LOG2E = math.log2(math.e)
MLA_Q_SCALE = (MLA_NOPE + MLA_ROPE) ** -0.5 * LOG2E
DIFF_Q_SCALE = DIFF_HD ** -0.5 * LOG2E

F32 = jnp.float32
BF16 = jnp.bfloat16


def _params(n_axes=1, vmem=VMEM_LIMIT):
    return pltpu.CompilerParams(dimension_semantics=("arbitrary",) * n_axes,
                                vmem_limit_bytes=vmem)


def _dot(a, b):
    return jnp.dot(a, b, preferred_element_type=F32)


def _rms(x, g):
    ms = jnp.mean(x * x, axis=-1, keepdims=True)
    return (x * lax.rsqrt(ms + EPS)) * g


def _modulate(x, g, shift, scale):
    return _rms(x, g) * (1.0 + scale) + shift


def _sigmoid(x):
    return 1.0 / (1.0 + jnp.exp(-x))


def _mod_parts(mod_ref):
    m = mod_ref[0]
    return [m[:, k * D:(k + 1) * D] for k in range(N_MOD)]


def _full(shape):
    return pl.BlockSpec(shape, lambda *_: (0,) * len(shape))


def _row_spec(width, rows=TM):
    return pl.BlockSpec((rows, width), lambda i: (i, 0))


def _split_specs(width, tm=TM):
    nbp = T_P // tm
    return [pl.BlockSpec((tm, width), lambda i: (jnp.minimum(i, nbp - 1), 0)),
            pl.BlockSpec((tm, width), lambda i: (jnp.maximum(i - nbp, 0), 0))]


def _split_rows(p_ref, s_ref, tm=TM):
    return jnp.where(pl.program_id(0) < T_P // tm, p_ref[...], s_ref[...])


def _mod_spec(layer, tm=TM):
    nbp, bps = T_P // tm, DEC_SEQ // tm

    def imap(i):
        j = jnp.maximum(i - nbp, 0)
        return (layer * MOD_ROWS + jnp.where(i < nbp, 0, 1 + j // bps), 0, 0)
    return pl.BlockSpec((1, 1, N_MOD * D), imap)


def _rope_spec(tm=TMX):
    nbp, bps = T_P // tm, DEC_SEQ // tm

    def imap(i):
        j = jnp.maximum(i - nbp, 0)
        return (jnp.where(i < nbp, bps, j % bps), 0)
    return pl.BlockSpec((tm, LANE), imap)


def _seq_edges(i, tm=TM):
    row0 = i * tm
    seq_len = jnp.where(row0 < T_P, SEQ, DEC_SEQ)
    return row0 % seq_len != 0, (row0 + tm) % seq_len != 0


def _group_rms(xb, gain, group_ones):
    sq = xb * xb
    hi = sq.astype(BF16)
    lo = (sq - hi.astype(F32)).astype(BF16)
    ms = (_dot(hi, group_ones) + _dot(lo, group_ones)) * (1.0 / 64.0)
    return (xb * lax.rsqrt(ms + EPS)) * gain


def _rope128(xb, cos, sin, first):
    swapped = jnp.where(first, pltpu.roll(xb, LANE - 32, 1), pltpu.roll(xb, 32, 1))
    return xb * cos + swapped * sin


ADA_TN = 1536


def _adaln_kernel(cond_ref, w_ref, b_ref, o_ref):
    c = cond_ref[...]
    a = (c * _sigmoid(c)).astype(BF16)
    o_ref[0] = _dot(a, w_ref[0].astype(BF16)) + b_ref[0]


def _adaln(cond, ada_w, ada_b):
    n = N_MOD * D
    return pl.pallas_call(
        _adaln_kernel,
        out_shape=jax.ShapeDtypeStruct((DEPTH, MOD_ROWS, n), F32),
        grid=(DEPTH, n // ADA_TN),
        in_specs=[pl.BlockSpec((MOD_ROWS, D), lambda l, j: (0, 0)),
                  pl.BlockSpec((1, D, ADA_TN), lambda l, j: (l, 0, j)),
                  pl.BlockSpec((1, 1, ADA_TN), lambda l, j: (l, 0, j))],
        out_specs=pl.BlockSpec((1, MOD_ROWS, ADA_TN), lambda l, j: (l, 0, j)),
        compiler_params=_params(2),
        name="adaln",
    )(cond, ada_w, ada_b.reshape(DEPTH, 1, n))


MLA_DOWN_W = MLA_Q_LORA + MLA_KV_LORA + LANE


def _mla_in_kernel(xp_ref, xs_ref, mod_ref, g1_ref, wd_ref, qn_ref, kvn_ref, knr_ref, wuq_ref,
                   qnn_ref, qnr_ref, cos_ref, sin_ref, q_ref, ckv_ref, kr_ref):
    shift, scale = _mod_parts(mod_ref)[:2]
    h = _modulate(_split_rows(xp_ref, xs_ref), g1_ref[...], shift, scale).astype(BF16)
    d = _dot(h, wd_ref[...])
    cq = _rms(d[:, :MLA_Q_LORA], qn_ref[...]).astype(BF16)
    ckv_ref[...] = _rms(d[:, MLA_Q_LORA:MLA_Q_LORA + MLA_KV_LORA], kvn_ref[...])

    lane = lax.broadcasted_iota(jnp.int32, (1, LANE), 1)
    first = (lane % 64) < 32
    cos, sin = cos_ref[...], sin_ref[...]

    def rope_part(xb, gain):
        ms = jnp.sum(xb * xb, axis=-1, keepdims=True) * (1.0 / MLA_ROPE)
        return _rope128((xb * lax.rsqrt(ms + EPS)) * gain, cos, sin, first)

    kr_ref[...] = rope_part(d[:, MLA_Q_LORA + MLA_KV_LORA:], knr_ref[...])

    q = _dot(cq, wuq_ref[...])
    for hh in range(MLA_HEADS):
        c0 = hh * MLA_DK
        qn = _rms(q[:, c0:c0 + LANE], qnn_ref[...])
        qr = rope_part(q[:, c0 + LANE:c0 + MLA_DK], qnr_ref[...])
        q_ref[:, c0:c0 + LANE] = (qn * MLA_Q_SCALE).astype(BF16)
        q_ref[:, c0 + LANE:c0 + MLA_DK] = (qr * MLA_Q_SCALE).astype(BF16)


def _mla_in(xp, xs, mods, layer, g1, wd, qn, kvn, knr, wuq, qnn, qnr, cos_t, sin_t):
    return pl.pallas_call(
        _mla_in_kernel,
        out_shape=(jax.ShapeDtypeStruct((T, MLA_HEADS * MLA_DK), BF16),
                   jax.ShapeDtypeStruct((T, MLA_KV_LORA), F32),
                   jax.ShapeDtypeStruct((T, LANE), F32)),
        grid=(NB,),
        in_specs=[*_split_specs(D), _mod_spec(layer), _full((1, D)), _full((D, MLA_DOWN_W)),
                  _full((1, MLA_Q_LORA)), _full((1, MLA_KV_LORA)), _full((1, LANE)),
                  _full((MLA_Q_LORA, MLA_HEADS * MLA_DK)), _full((1, LANE)), _full((1, LANE)),
                  _rope_spec(TM), _rope_spec(TM)],
        out_specs=(_row_spec(MLA_HEADS * MLA_DK), _row_spec(MLA_KV_LORA), _row_spec(LANE)),
        compiler_params=_params(),
        name="mla_in",
    )(xp, xs, mods, g1, wd, qn, kvn, knr, wuq, qnn, qnr, cos_t, sin_t)


def _mla_kv_kernel(ckv_ref, kr_ref, wuk_ref, wuv_ref, knn_ref, kt_ref, v_ref):
    c = ckv_ref[...].astype(BF16)
    kn = _dot(c, wuk_ref[...])
    v_ref[...] = _dot(c, wuv_ref[...]).astype(BF16)
    kr_t = kr_ref[...].T.astype(BF16)
    for hh in range(MLA_HEADS):
        r0 = hh * MLA_DK
        kh = _rms(kn[:, hh * LANE:(hh + 1) * LANE], knn_ref[...])
        kt_ref[r0:r0 + LANE, :] = kh.T.astype(BF16)
        kt_ref[r0 + LANE:r0 + MLA_DK, :] = kr_t


def _mla_kv(ckv, kr, wuk, wuv, knn):
    n = ckv.shape[0]
    return pl.pallas_call(
        _mla_kv_kernel,
        out_shape=(jax.ShapeDtypeStruct((MLA_HEADS * MLA_DK, n), BF16),
                   jax.ShapeDtypeStruct((n, MLA_HEADS * MLA_V), BF16)),
        grid=(n // TMG,),
        in_specs=[_row_spec(MLA_KV_LORA, TMG), _row_spec(LANE, TMG),
                  _full((MLA_KV_LORA, MLA_HEADS * MLA_NOPE)), _full((MLA_KV_LORA, MLA_HEADS * MLA_V)),
                  _full((1, LANE))],
        out_specs=(pl.BlockSpec((MLA_HEADS * MLA_DK, TMG), lambda i: (0, i)),
                   _row_spec(MLA_HEADS * MLA_V, TMG)),
        compiler_params=_params(),
        name="mla_kv",
    )(ckv, kr, wuk, wuv, knn)


def _scores(q, kts, rows):
    parts = [_dot(q, kt_ref[rows, :]) for kt_ref in kts]
    return parts[0] if len(parts) == 1 else jnp.concatenate(parts, axis=1)


def _softmax_pv(s2, vs, cols):
    p = jnp.exp2(s2 - jnp.max(s2, axis=-1, keepdims=True)).astype(BF16)
    o, k0 = None, 0
    for v_ref in vs:
        v = v_ref[:, cols]
        part = _dot(p[:, k0:k0 + v.shape[0]], jnp.concatenate([v, jnp.ones_like(v)], axis=1))
        o = part if o is None else o + part
        k0 += v.shape[0]
    return o[:, :LANE] * (1.0 / o[:, LANE:LANE + 1])


def _mla_heads(q_ref, kts, vs, o_ref):
    for hh in range(MLA_HEADS):
        ks = slice(hh * MLA_DK, (hh + 1) * MLA_DK)
        cs = slice(hh * MLA_V, (hh + 1) * MLA_V)
        o_ref[:, cs] = _softmax_pv(_scores(q_ref[:, ks], kts, ks), vs, cs).astype(BF16)


def _diff_heads(lam_init, lq1_ref, lk1_ref, lq2_ref, lk2_ref, hn_ref, q_ref, kts, vs, o_ref):
    lam = (jnp.exp(jnp.sum(lq1_ref[...] * lk1_ref[...], axis=-1, keepdims=True))
           - jnp.exp(jnp.sum(lq2_ref[...] * lk2_ref[...], axis=-1, keepdims=True)) + lam_init)
    lane = lax.broadcasted_iota(jnp.int32, (1, LANE), 1)
    left = lane < DIFF_HD
    zero = jnp.zeros((), BF16)
    for hh in range(DIFF_HEADS):
        cs = slice(hh * LANE, (hh + 1) * LANE)
        qb = q_ref[:, cs]
        o0 = _softmax_pv(_scores(jnp.where(left, qb, zero), kts, cs), vs, cs)
        o1 = _softmax_pv(_scores(jnp.where(left, zero, qb), kts, cs), vs, cs)
        o = o0 - lam * o1
        o_ref[:, cs] = (_rms(o, hn_ref[...]) * (1.0 - lam_init)).astype(BF16)


def _residual_and_ffn_in(x, mix, mods, g2):
    _, _, gate1, shift2, scale2, _ = mods
    y = x + gate1 * mix
    return y, _modulate(y, g2, shift2, scale2).astype(BF16)


def _attn_kernel(heads, n_x, n_small, *refs):
    x_refs, refs = refs[:n_x], refs[n_x:]
    q_ref, ktp_ref, ktn_ref, ktc_ref, vp_ref, vn_ref, vc_ref = refs[:7]
    smalls, (mod_ref, wo_ref, g2_ref, y_ref, hf_ref, o_ref) = refs[7:7 + n_small], refs[7 + n_small:]
    is_prompt = pl.program_id(0) < NB_P

    @pl.when(is_prompt)
    def _():
        heads(*smalls, q_ref, [ktp_ref], [vp_ref], o_ref)

    @pl.when(jnp.logical_not(is_prompt))
    def _():
        heads(*smalls, q_ref, [ktc_ref, ktn_ref], [vc_ref, vn_ref], o_ref)

    x = x_refs[0][...] if n_x == 1 else _split_rows(*x_refs)
    y, hf = _residual_and_ffn_in(x, _dot(o_ref[...], wo_ref[...]), _mod_parts(mod_ref), g2_ref[...])
    y_ref[...] = y
    hf_ref[...] = hf


def _attention(heads, name, xs, q, kt, v, kt_cache, v_cache, smalls, mods, layer, wo, g2, dk, dv):
    assert SEQ == TM and T_P % DEC_SEQ == 0
    pro = lambda i: jnp.minimum(i, NB_P - 1)
    lat = lambda i: jnp.maximum(i - NB_P, 0) // BPS
    lat_new = lambda i: T_P // DEC_SEQ + lat(i)
    x_specs = [_row_spec(D)] if len(xs) == 1 else _split_specs(D)
    return pl.pallas_call(
        functools.partial(_attn_kernel, heads, len(xs), len(smalls)),
        out_shape=(jax.ShapeDtypeStruct((T, D), F32), jax.ShapeDtypeStruct((T, D), BF16)),
        grid=(NB,),
        in_specs=[*x_specs, _row_spec(dk),
                  pl.BlockSpec((dk, SEQ), lambda i: (0, pro(i))),
                  pl.BlockSpec((dk, DEC_SEQ), lambda i: (0, lat_new(i))),
                  pl.BlockSpec((dk, PAST), lambda i: (0, lat(i))),
                  pl.BlockSpec((SEQ, dv), lambda i: (pro(i), 0)),
                  pl.BlockSpec((DEC_SEQ, dv), lambda i: (lat_new(i), 0)),
                  pl.BlockSpec((PAST, dv), lambda i: (lat(i), 0)),
                  *[_full(s.shape) for s in smalls], _mod_spec(layer), _full((dv, D)), _full((1, D))],
        out_specs=(_row_spec(D), _row_spec(D)),
        scratch_shapes=[pltpu.VMEM((TM, dv), BF16)],
        compiler_params=_params(),
        name=name,
    )(*xs, q, kt, kt, kt_cache, v, v, v_cache, *smalls, mods, wo, g2)


def _zero_row(a, r, cond):
    r0 = r // HALO_F32 * HALO_F32
    tile = a[r0:r0 + HALO_F32]
    row = lax.broadcasted_iota(jnp.int32, (HALO_F32, 1), 0)
    tile = jnp.where(row == r - r0, jnp.where(cond, 0.0, tile), tile)
    return jnp.concatenate([a[:r0], tile, a[r0 + HALO_F32:]], axis=0)


def _conv3(zs_ref, w, off, tm=TM, is_prompt=False):
    z = zs_ref[...]
    n = z.shape[0]
    z_prev = pltpu.roll(z, 1, 0)[off:off + tm]
    z_next = pltpu.roll(z, n - 1, 0)[off:off + tm]
    for start in range(SEQ, tm, SEQ):
        z_prev = _zero_row(z_prev, start, is_prompt)
        z_next = _zero_row(z_next, start - 1, is_prompt)
    return z_prev * w[0:1] + z[off:off + tm] * w[1:2] + z_next * w[2:3]


def _ffn_kernel(n_out, y_ref, hf_ref, prev_ref, next_ref, mod_ref, win_ref, cw_ref, cb_ref, wout_ref,
                *rest):
    out_refs, (lhs_ref, z_ref, act_ref) = rest[:n_out], rest[n_out:]
    tm, hb = FFN_TM, HALO_BF16
    is_prompt = pl.program_id(0) < T_P // tm
    has_prev, has_next = _seq_edges(pl.program_id(0), tm)
    zeros = jnp.zeros((hb, D), BF16)
    lhs_ref[0:hb, :] = jnp.where(has_prev, prev_ref[...], zeros)
    lhs_ref[hb:hb + tm, :] = hf_ref[...]
    lhs_ref[hb + tm:, :] = jnp.where(has_next, next_ref[...], zeros)
    lhs = lhs_ref[...]
    for c in range(FFN_HIDDEN // HC):
        gs = slice(c * HC, (c + 1) * HC)
        us = slice(FFN_HIDDEN + c * HC, FFN_HIDDEN + (c + 1) * HC)
        z_ref[:, gs] = _dot(lhs, win_ref[0, :, gs])
        z_ref[:, us] = _dot(lhs, win_ref[0, :, us])
        g = _conv3(z_ref.at[:, gs], cw_ref[0, :, gs], hb, tm, is_prompt) + cb_ref[0, :, gs]
        u = _conv3(z_ref.at[:, us], cw_ref[0, :, us], hb, tm, is_prompt) + cb_ref[0, :, us]
        act_ref[:, gs] = ((g * _sigmoid(g)) * u).astype(BF16)
    gate2 = _mod_parts(mod_ref)[5]
    out = y_ref[...] + gate2 * _dot(act_ref[...], wout_ref[0])
    if len(out_refs) == 1:
        out_refs[0][...] = out
    else:
        @pl.when(is_prompt)
        def _():
            out_refs[0][...] = out

        @pl.when(jnp.logical_not(is_prompt))
        def _():
            out_refs[1][...] = out


def _ffn(y, hf, mods, layer, win, cw, cb, wout, split_out=False):
    tm, hb = FFN_TM, HALO_BF16
    r = tm // hb
    if split_out:
        out_shape = (jax.ShapeDtypeStruct((T_P, D), F32), jax.ShapeDtypeStruct((T_S, D), F32))
        out_specs = tuple(_split_specs(D, tm))
    else:
        out_shape, out_specs = (jax.ShapeDtypeStruct((T, D), F32),), (_row_spec(D, tm),)
    w_specs = [pl.BlockSpec((1,) + w.shape[1:], lambda i: (layer, 0, 0), pipeline_mode=pl.Buffered(1))
               for w in (win, cw, cb, wout)]
    return pl.pallas_call(
        functools.partial(_ffn_kernel, len(out_shape)),
        out_shape=out_shape,
        grid=(T // tm,),
        in_specs=[_row_spec(D, tm), _row_spec(D, tm),
                  pl.BlockSpec((hb, D), lambda i: (jnp.maximum(i * r - 1, 0), 0)),
                  pl.BlockSpec((hb, D), lambda i: (jnp.minimum((i + 1) * r, T // hb - 1), 0)),
                  _mod_spec(layer, tm), *w_specs],
        out_specs=out_specs,
        scratch_shapes=[pltpu.VMEM((tm + 2 * hb, D), BF16),
                        pltpu.VMEM((tm + 2 * hb, 2 * FFN_HIDDEN), F32),
                        pltpu.VMEM((tm, FFN_HIDDEN), BF16)],
        compiler_params=_params(),
        name="ffn",
    )(y, hf, hf, hf, mods, win, cw, cb, wout)


def _diff_in_kernel(x_ref, mod_ref, g1_ref, w_ref, qn_ref, kn_ref, cos_ref, sin_ref,
                    q_ref, kt_ref, v_ref, kst_ref, vst_ref, kn_ref_scratch):
    shift, scale = _mod_parts(mod_ref)[:2]
    h = _modulate(x_ref[...], g1_ref[...], shift, scale).astype(BF16)
    z = _dot(h, w_ref[...])
    lane = lax.broadcasted_iota(jnp.int32, (1, LANE), 1)
    first = (lane % 64) < 32
    same_half = (lax.broadcasted_iota(jnp.int32, (LANE, LANE), 0) < DIFF_HD) == (
        lax.broadcasted_iota(jnp.int32, (LANE, LANE), 1) < DIFF_HD)
    group_ones = jnp.where(same_half, 1.0, 0.0).astype(BF16)
    cos, sin = cos_ref[...], sin_ref[...]
    v = z[:, 2 * D:]
    v_ref[...] = v.astype(BF16)
    for hh in range(DIFF_HEADS):
        cs = slice(hh * LANE, (hh + 1) * LANE)
        qn = _group_rms(z[:, hh * LANE:(hh + 1) * LANE], qn_ref[...], group_ones)
        q_ref[:, cs] = (_rope128(qn, cos, sin, first) * DIFF_Q_SCALE).astype(BF16)
        kn = _group_rms(z[:, D + hh * LANE:D + (hh + 1) * LANE], kn_ref[...], group_ones)
        kt_ref[hh * LANE:(hh + 1) * LANE, :] = _rope128(kn, cos, sin, first).T.astype(BF16)
        kn_ref_scratch[:, cs] = kn

    @pl.when(pl.program_id(0) < T_P // TMX)
    def _():
        vst_ref[...] = v
        kst_ref[...] = kn_ref_scratch[...]


def _diff_in(x, mods, layer, g1, w, qn, kn, cos_t, sin_t):
    st_spec = _split_specs(D, TMX)[0]
    return pl.pallas_call(
        _diff_in_kernel,
        out_shape=(jax.ShapeDtypeStruct((T, D), BF16), jax.ShapeDtypeStruct((D, T), BF16),
                   jax.ShapeDtypeStruct((T, D), BF16), jax.ShapeDtypeStruct((T_P, D), F32),
                   jax.ShapeDtypeStruct((T_P, D), F32)),
        grid=(T // TMX,),
        in_specs=[_row_spec(D, TMX), _mod_spec(layer, TMX), _full((1, D)), _full((D, 3 * D)),
                  _full((1, LANE)), _full((1, LANE)), _rope_spec(), _rope_spec()],
        out_specs=(_row_spec(D, TMX), pl.BlockSpec((D, TMX), lambda i: (0, i)), _row_spec(D, TMX),
                   st_spec, st_spec),
        scratch_shapes=[pltpu.VMEM((TMX, D), F32)],
        compiler_params=_params(),
        name="diff_in",
    )(x, mods, g1, w, qn, kn, cos_t, sin_t)


def _sconv_kernel(x_ref, prev_ref, next_ref, mod_ref, g1_ref, win_ref, cw_ref, wout_ref, g2_ref,
                  y_ref, hf_ref, p_ref, mixed_ref):
    tm, hb = TMG, HALO_F32
    is_prompt = pl.program_id(0) < T_P // tm
    has_prev, has_next = _seq_edges(pl.program_id(0), tm)
    mods = _mod_parts(mod_ref)
    x = x_ref[...]
    mod_in = lambda a: _modulate(a, g1_ref[...], mods[0], mods[1])
    zeros = jnp.zeros((hb, D), F32)
    lhs = jnp.concatenate([jnp.where(has_prev, mod_in(prev_ref[...]), zeros), mod_in(x),
                           jnp.where(has_next, mod_in(next_ref[...]), zeros)],
                          axis=0).astype(BF16)
    for c in range(D // HC):
        cs = slice(c * HC, (c + 1) * HC)
        gb = _dot(lhs, win_ref[:, c * HC:(c + 1) * HC])
        gc = _dot(lhs, win_ref[:, D + c * HC:D + (c + 1) * HC])
        u = _dot(lhs, win_ref[:, 2 * D + c * HC:2 * D + (c + 1) * HC])
        p_ref[:, cs] = gc * u
        conv = _conv3(p_ref.at[:, cs], cw_ref[:, cs], hb, tm, is_prompt)
        mixed_ref[:, cs] = (gb[hb:hb + tm] * conv).astype(BF16)
    y, hf = _residual_and_ffn_in(x, _dot(mixed_ref[...], wout_ref[...]), mods, g2_ref[...])
    y_ref[...] = y
    hf_ref[...] = hf


def _sconv(x, mods, layer, g1, win, cw, wout, g2):
    tm, hb = TMG, HALO_F32
    r = tm // hb
    once = dict(pipeline_mode=pl.Buffered(1))
    return pl.pallas_call(
        _sconv_kernel,
        out_shape=(jax.ShapeDtypeStruct((T, D), F32), jax.ShapeDtypeStruct((T, D), BF16)),
        grid=(T // tm,),
        in_specs=[_row_spec(D, tm),
                  pl.BlockSpec((hb, D), lambda i: (jnp.maximum(i * r - 1, 0), 0)),
                  pl.BlockSpec((hb, D), lambda i: (jnp.minimum((i + 1) * r, T // hb - 1), 0)),
                  _mod_spec(layer, tm), _full((1, D)),
                  pl.BlockSpec((D, 3 * D), lambda i: (0, 0), **once), _full((3, D)),
                  pl.BlockSpec((D, D), lambda i: (0, 0), **once), _full((1, D))],
        out_specs=(_row_spec(D, tm), _row_spec(D, tm)),
        scratch_shapes=[pltpu.VMEM((tm + 2 * hb, D), F32), pltpu.VMEM((tm, D), BF16)],
        compiler_params=_params(),
        name="sconv",
    )(x, x, x, mods, g1, win, cw, wout, g2)


def _gmlp_kernel(x_ref, mod_ref, g1_ref, win_ref, vn_ref, ws_ref, bs_ref, wout_ref, g2_ref,
                 y_ref, hf_ref, gated_ref):
    mods = _mod_parts(mod_ref)
    x = x_ref[...]
    h = _modulate(x, g1_ref[...], mods[0], mods[1]).astype(BF16)
    z = _dot(h, win_ref[...])
    z = z * (0.5 * (1.0 + jnp.tanh(math.sqrt(2.0 / math.pi) * (z + 0.044715 * (z * z * z)))))
    u = z[:, :D]
    v = _rms(z[:, D:], vn_ref[...]).astype(BF16)
    bs = bs_ref[...]
    for g in range(GMLP_GROUPS):
        cs = slice(g * LANE, (g + 1) * LANE)
        w = ws_ref[g]
        for r in range(TMG // GMLP_CHUNK):
            rs = slice(r * GMLP_CHUNK, (r + 1) * GMLP_CHUNK)
            mixed = _dot(w, v[rs, cs]) + bs[:, g:g + 1]
            gated_ref[rs, cs] = (u[rs, cs] * mixed).astype(BF16)
    y, hf = _residual_and_ffn_in(x, _dot(gated_ref[...], wout_ref[...]), mods, g2_ref[...])
    y_ref[...] = y
    hf_ref[...] = hf


def _gmlp(x, mods, layer, g1, win, vn, ws, bs_t, wout, g2):
    return pl.pallas_call(
        _gmlp_kernel,
        out_shape=(jax.ShapeDtypeStruct((T, D), F32), jax.ShapeDtypeStruct((T, D), BF16)),
        grid=(T // TMG,),
        in_specs=[_row_spec(D, TMG), _mod_spec(layer, TMG), _full((1, D)), _full((D, 2 * D)), _full((1, D)),
                  _full((GMLP_GROUPS, GMLP_CHUNK, GMLP_CHUNK)), _full((GMLP_CHUNK, GMLP_GROUPS)),
                  _full((D, D)), _full((1, D))],
        out_specs=(_row_spec(D, TMG), _row_spec(D, TMG)),
        scratch_shapes=[pltpu.VMEM((TMG, D), BF16)],
        compiler_params=_params(),
        name="gmlp",
    )(x, mods, g1, win, vn, ws, bs_t, wout, g2)


def _rope_tables():
    rows = DEC_SEQ // GRID_W
    row = jnp.repeat(jnp.arange(rows, dtype=F32), GRID_W)
    col = jnp.tile(jnp.arange(GRID_W, dtype=F32), rows)
    n_freq = MLA_ROPE // 4
    inv_freq = ROPE_THETA ** (-jnp.arange(n_freq, dtype=F32) / n_freq)
    ang = jnp.concatenate([row[:, None] * inv_freq, col[:, None] * inv_freq], axis=-1)
    cos, sin = jnp.cos(ang), jnp.sin(ang)
    cos = jnp.concatenate([cos, jnp.ones((TMX, cos.shape[1]), F32)], axis=0)
    sin = jnp.concatenate([sin, jnp.zeros((TMX, sin.shape[1]), F32)], axis=0)
    return jnp.tile(cos, (1, 4)), jnp.tile(jnp.concatenate([-sin, sin], axis=1), (1, 2))


def _pad_lanes(a, width=LANE):
    return jnp.pad(a, [(0, 0)] * (a.ndim - 1) + [(0, width - a.shape[-1])])


def _heads_first(a):
    return jnp.swapaxes(a, -3, -2).reshape(a.shape[:-3] + (D,))


def kernel(x_prompt, x_sample, cache_mla_ckv, cache_mla_krope, cache_diff_k, cache_diff_v, c, c_ctx, ada_w, ada_b, norm1_g, norm2_g, mla_w_down, mla_q_norm, mla_kv_norm, mla_w_uq, mla_w_uk, mla_w_uv, mla_qn_nope, mla_qn_rope, mla_kn_nope, mla_kn_rope, mla_w_o, diff_w_qkv, diff_qn, diff_kn, diff_lq1, diff_lk1, diff_lq2, diff_lk2, diff_head_norm, diff_w_o, sconv_w_in, sconv_w, sconv_w_out, gmlp_w_in, gmlp_v_norm, gmlp_w_s, gmlp_b_s, gmlp_w_out, ffn_w_in, ffn_conv_w, ffn_conv_b, ffn_w_out):
    xp, xs = x_prompt.reshape(T_P, D), x_sample.reshape(T_S, D)
    cond = jnp.concatenate([c_ctx[None], c, jnp.zeros((MOD_ROWS - 1 - DEC_BATCH, D), F32)], axis=0)
    mods = _adaln(cond, ada_w, ada_b).reshape(DEPTH * MOD_ROWS, 1, N_MOD * D)
    cos_t, sin_t = _rope_tables()
    row = lambda a: a.reshape(1, -1)

    ffn_w = (ffn_w_in.astype(BF16), ffn_conv_w, ffn_conv_b[:, None, :], ffn_w_out.astype(BF16))

    def ffn(y, hf, i):
        return _ffn(y, hf, mods, i, *ffn_w, split_out=(i == DEPTH - 1))

    wd = _pad_lanes(mla_w_down[0], MLA_DOWN_W).astype(BF16)
    wuq = _pad_lanes(mla_w_uq[0].reshape(MLA_Q_LORA, MLA_HEADS, MLA_NOPE + MLA_ROPE), MLA_DK)
    wuq = wuq.reshape(MLA_Q_LORA, MLA_HEADS * MLA_DK).astype(BF16)
    q, ckv, kr = _mla_in(xp, xs, mods, 0, row(norm1_g[0]), wd, row(mla_q_norm[0]), row(mla_kv_norm[0]),
                         _pad_lanes(row(mla_kn_rope[0])), wuq, row(mla_qn_nope[0]),
                         _pad_lanes(row(mla_qn_rope[0])), cos_t, sin_t)
    state_ckv = ckv[:T_P].reshape(BATCH, 1, SEQ, MLA_KV_LORA)
    state_kr = kr[:T_P, :MLA_ROPE].reshape(BATCH, 1, SEQ, MLA_ROPE)
    kv_w = (mla_w_uk[0].astype(BF16), mla_w_uv[0].astype(BF16), row(mla_kn_nope[0]))
    kt, v = _mla_kv(ckv, kr, *kv_w)
    kt_cache, v_cache = _mla_kv(cache_mla_ckv[:, 0].reshape(DEC_BATCH * PAST, MLA_KV_LORA),
                                _pad_lanes(cache_mla_krope[:, 0]).reshape(DEC_BATCH * PAST, LANE), *kv_w)
    y, hf = _attention(_mla_heads, "mla_attn", (xp, xs), q, kt, v, kt_cache, v_cache, [], mods, 0,
                       mla_w_o[0].astype(BF16), row(norm2_g[0]), MLA_HEADS * MLA_DK, MLA_HEADS * MLA_V)
    x, = ffn(y, hf, 0)

    lam_init = 0.8 - 0.6 * math.exp(-0.3 * 1)
    wq, wk, wv = jnp.split(diff_w_qkv[0], 3, axis=-1)
    perm = lambda w: _heads_first(w.reshape(D, 2, DIFF_HEADS, DIFF_HD))
    w_qkv = jnp.concatenate([perm(wq), perm(wk), wv], axis=-1).astype(BF16)
    pair = lambda g: row(jnp.tile(g, 2))
    q, kt, v, kst, vst = _diff_in(x, mods, 1, row(norm1_g[1]), w_qkv, pair(diff_qn[0]), pair(diff_kn[0]),
                                 cos_t, sin_t)
    state_dk = jnp.swapaxes(kst.reshape(BATCH, SEQ, DIFF_HEADS, 2, DIFF_HD), 2, 3)[:, None]
    state_dv = vst.reshape(BATCH, 1, SEQ, DIFF_HEADS, 2 * DIFF_HD)
    kt_cache = _heads_first(cache_diff_k[:, 0]).astype(BF16).reshape(DEC_BATCH * PAST, D).T
    v_cache = cache_diff_v[:, 0].reshape(DEC_BATCH * PAST, D).astype(BF16)
    smalls = [row(diff_lq1[0]), row(diff_lk1[0]), row(diff_lq2[0]), row(diff_lk2[0]),
              row(diff_head_norm[0])]
    y, hf = _attention(functools.partial(_diff_heads, lam_init), "diff_attn", (x,), q, kt, v, kt_cache, v_cache,
                       smalls, mods, 1, diff_w_o[0].astype(BF16), row(norm2_g[1]), D, D)
    x, = ffn(y, hf, 1)

    y, hf = _sconv(x, mods, 2, row(norm1_g[2]), sconv_w_in[0].astype(BF16), sconv_w[0],
                   sconv_w_out[0].astype(BF16), row(norm2_g[2]))
    x, = ffn(y, hf, 2)

    y, hf = _gmlp(x, mods, 3, row(norm1_g[3]), gmlp_w_in[0].astype(BF16), row(gmlp_v_norm[0]),
                  gmlp_w_s[0].astype(BF16), gmlp_b_s[0].T, gmlp_w_out[0].astype(BF16), row(norm2_g[3]))
    yp, ys = ffn(y, hf, 3)

    return (yp.reshape(BATCH, SEQ, D), ys.reshape(DEC_BATCH, DEC_SEQ, D),
            state_ckv, state_kr, state_dk, state_dv)
```

```python
import functools
import math

import jax
import jax.numpy as jnp
from jax import lax
from jax.experimental import pallas as pl
from jax.experimental.pallas import tpu as pltpu

D = 1024
BATCH, SEQ = 16, 256
DEC_BATCH, DEC_SEQ = 8, 2048
PAST = 256
DEPTH = 4
GRID_W = 64
EPS = 1e-6
ROPE_THETA = 10000.0
N_MOD = 6
MLA_HEADS, MLA_NOPE, MLA_ROPE, MLA_V = 8, 128, 64, 128
MLA_Q_LORA, MLA_KV_LORA = 768, 256
DIFF_HEADS, DIFF_HD = 8, 64
GMLP_CHUNK, GMLP_GROUPS = 128, 8
FFN_HIDDEN = 2816

T_P = BATCH * SEQ
T_S = DEC_BATCH * DEC_SEQ
T = T_P + T_S
TM = 256
TMX = 512
TMG = 1024
FFN_TM = 512
NB_P = T_P // TM
NB = T // TM
BPS = DEC_SEQ // TM
MOD_ROWS = 16
LANE = 128
MLA_DK = 256
HC = 256
HALO_BF16 = 16
HALO_F32 = 8
VMEM_LIMIT = 56 * 1024 * 1024
LOG2E = math.log2(math.e)
MLA_Q_SCALE = (MLA_NOPE + MLA_ROPE) ** -0.5 * LOG2E
DIFF_Q_SCALE = DIFF_HD ** -0.5 * LOG2E

F32 = jnp.float32
BF16 = jnp.bfloat16


def _params(n_axes=1, vmem=VMEM_LIMIT):
    return pltpu.CompilerParams(dimension_semantics=("arbitrary",) * n_axes,
                                vmem_limit_bytes=vmem)


def _dot(a, b):
    return jnp.dot(a, b, preferred_element_type=F32)


def _rms(x, g):
    ms = jnp.mean(x * x, axis=-1, keepdims=True)
    return (x * lax.rsqrt(ms + EPS)) * g


def _modulate(x, g, shift, scale):
    return _rms(x, g) * (1.0 + scale) + shift


def _sigmoid(x):
    return 1.0 / (1.0 + jnp.exp(-x))


def _mod_parts(mod_ref):
    m = mod_ref[0]
    return [m[:, k * D:(k + 1) * D] for k in range(N_MOD)]


def _full(shape):
    return pl.BlockSpec(shape, lambda *_: (0,) * len(shape), pipeline_mode=pl.Buffered(1))


def _row_spec(width, rows=TM):
    return pl.BlockSpec((rows, width), lambda i: (i, 0))


def _split_specs(width, tm=TM):
    nbp = T_P // tm
    return [pl.BlockSpec((tm, width), lambda i: (jnp.minimum(i, nbp - 1), 0)),
            pl.BlockSpec((tm, width), lambda i: (jnp.maximum(i - nbp, 0), 0))]


def _split_rows(p_ref, s_ref, tm=TM):
    return jnp.where(pl.program_id(0) < T_P // tm, p_ref[...], s_ref[...])


def _mod_spec(layer, tm=TM):
    nbp, bps = T_P // tm, DEC_SEQ // tm

    def imap(i):
        j = jnp.maximum(i - nbp, 0)
        return (layer * MOD_ROWS + jnp.where(i < nbp, 0, 1 + j // bps), 0, 0)
    return pl.BlockSpec((1, 1, N_MOD * D), imap)


def _rope_spec(tm=TMX):
    nbp, bps = T_P // tm, DEC_SEQ // tm

    def imap(i):
        j = jnp.maximum(i - nbp, 0)
        return (jnp.where(i < nbp, bps, j % bps), 0)
    return pl.BlockSpec((tm, LANE), imap)


def _seq_edges(i, tm=TM):
    row0 = i * tm
    seq_len = jnp.where(row0 < T_P, SEQ, DEC_SEQ)
    return row0 % seq_len != 0, (row0 + tm) % seq_len != 0


def _group_rms(xb, gain, group_ones):
    sq = xb * xb
    hi = sq.astype(BF16)
    lo = (sq - hi.astype(F32)).astype(BF16)
    ms = (_dot(hi, group_ones) + _dot(lo, group_ones)) * (1.0 / 64.0)
    return (xb * lax.rsqrt(ms + EPS)) * gain


def _rope128(xb, cos, sin, first):
    swapped = jnp.where(first, pltpu.roll(xb, LANE - 32, 1), pltpu.roll(xb, 32, 1))
    return xb * cos + swapped * sin


ADA_TN = 1536


def _adaln_kernel(cond_ref, w_ref, b_ref, o_ref):
    c = cond_ref[...]
    a = (c * _sigmoid(c)).astype(BF16)
    o_ref[0] = _dot(a, w_ref[0].astype(BF16)) + b_ref[0]


def _adaln(cond, ada_w, ada_b):
    n = N_MOD * D
    return pl.pallas_call(
        _adaln_kernel,
        out_shape=jax.ShapeDtypeStruct((DEPTH, MOD_ROWS, n), F32),
        grid=(DEPTH, n // ADA_TN),
        in_specs=[pl.BlockSpec((MOD_ROWS, D), lambda l, j: (0, 0)),
                  pl.BlockSpec((1, D, ADA_TN), lambda l, j: (l, 0, j)),
                  pl.BlockSpec((1, 1, ADA_TN), lambda l, j: (l, 0, j))],
        out_specs=pl.BlockSpec((1, MOD_ROWS, ADA_TN), lambda l, j: (l, 0, j)),
        compiler_params=_params(2),
        name="adaln",
    )(cond, ada_w, ada_b.reshape(DEPTH, 1, n))


MLA_DOWN_W = MLA_Q_LORA + MLA_KV_LORA + LANE


def _mla_in_kernel(xp_ref, xs_ref, mod_ref, g1_ref, wd_ref, qn_ref, kvn_ref, knr_ref, wuq_ref,
                   qnn_ref, qnr_ref, cos_ref, sin_ref, q_ref, ckv_ref, kr_ref):
    shift, scale = _mod_parts(mod_ref)[:2]
    h = _modulate(_split_rows(xp_ref, xs_ref), g1_ref[...], shift, scale).astype(BF16)
    d = _dot(h, wd_ref[...])
    cq = _rms(d[:, :MLA_Q_LORA], qn_ref[...]).astype(BF16)
    ckv_ref[...] = _rms(d[:, MLA_Q_LORA:MLA_Q_LORA + MLA_KV_LORA], kvn_ref[...])

    lane = lax.broadcasted_iota(jnp.int32, (1, LANE), 1)
    first = (lane % 64) < 32
    cos, sin = cos_ref[...], sin_ref[...]

    def rope_part(xb, gain):
        ms = jnp.sum(xb * xb, axis=-1, keepdims=True) * (1.0 / MLA_ROPE)
        return _rope128((xb * lax.rsqrt(ms + EPS)) * gain, cos, sin, first)

    kr_ref[...] = rope_part(d[:, MLA_Q_LORA + MLA_KV_LORA:], knr_ref[...])

    q = _dot(cq, wuq_ref[...])
    for hh in range(MLA_HEADS):
        c0 = hh * MLA_DK
        qn = _rms(q[:, c0:c0 + LANE], qnn_ref[...])
        qr = rope_part(q[:, c0 + LANE:c0 + MLA_DK], qnr_ref[...])
        q_ref[:, c0:c0 + LANE] = (qn * MLA_Q_SCALE).astype(BF16)
        q_ref[:, c0 + LANE:c0 + MLA_DK] = (qr * MLA_Q_SCALE).astype(BF16)


def _mla_in(xp, xs, mods, layer, g1, wd, qn, kvn, knr, wuq, qnn, qnr, cos_t, sin_t):
    return pl.pallas_call(
        _mla_in_kernel,
        out_shape=(jax.ShapeDtypeStruct((T, MLA_HEADS * MLA_DK), BF16),
                   jax.ShapeDtypeStruct((T, MLA_KV_LORA), F32),
                   jax.ShapeDtypeStruct((T, LANE), F32)),
        grid=(NB,),
        in_specs=[*_split_specs(D), _mod_spec(layer), _full((1, D)), _full((D, MLA_DOWN_W)),
                  _full((1, MLA_Q_LORA)), _full((1, MLA_KV_LORA)), _full((1, LANE)),
                  _full((MLA_Q_LORA, MLA_HEADS * MLA_DK)), _full((1, LANE)), _full((1, LANE)),
                  _rope_spec(TM), _rope_spec(TM)],
        out_specs=(_row_spec(MLA_HEADS * MLA_DK), _row_spec(MLA_KV_LORA), _row_spec(LANE)),
        compiler_params=_params(),
        name="mla_in",
    )(xp, xs, mods, g1, wd, qn, kvn, knr, wuq, qnn, qnr, cos_t, sin_t)


def _mla_kv_kernel(ckv_ref, kr_ref, wuk_ref, wuv_ref, knn_ref, kt_ref, v_ref):
    c = ckv_ref[...].astype(BF16)
    kn = _dot(c, wuk_ref[...])
    v_ref[...] = _dot(c, wuv_ref[...]).astype(BF16)
    kr_t = kr_ref[...].T.astype(BF16)
    for hh in range(MLA_HEADS):
        r0 = hh * MLA_DK
        kh = _rms(kn[:, hh * LANE:(hh + 1) * LANE], knn_ref[...])
        kt_ref[r0:r0 + LANE, :] = kh.T.astype(BF16)
        kt_ref[r0 + LANE:r0 + MLA_DK, :] = kr_t


def _mla_kv(ckv, kr, wuk, wuv, knn):
    n = ckv.shape[0]
    return pl.pallas_call(
        _mla_kv_kernel,
        out_shape=(jax.ShapeDtypeStruct((MLA_HEADS * MLA_DK, n), BF16),
                   jax.ShapeDtypeStruct((n, MLA_HEADS * MLA_V), BF16)),
        grid=(n // TMG,),
        in_specs=[_row_spec(MLA_KV_LORA, TMG), _row_spec(LANE, TMG),
                  _full((MLA_KV_LORA, MLA_HEADS * MLA_NOPE)), _full((MLA_KV_LORA, MLA_HEADS * MLA_V)),
                  _full((1, LANE))],
        out_specs=(pl.BlockSpec((MLA_HEADS * MLA_DK, TMG), lambda i: (0, i)),
                   _row_spec(MLA_HEADS * MLA_V, TMG)),
        compiler_params=_params(),
        name="mla_kv",
    )(ckv, kr, wuk, wuv, knn)


def _scores(q, kts, rows):
    parts = [_dot(q, kt_ref[rows, :]) for kt_ref in kts]
    return parts[0] if len(parts) == 1 else jnp.concatenate(parts, axis=1)


def _softmax_pv(s2, vs, cols):
    p = jnp.exp2(s2 - jnp.max(s2, axis=-1, keepdims=True)).astype(BF16)
    o, k0 = None, 0
    for v_ref in vs:
        v = v_ref[:, cols]
        part = _dot(p[:, k0:k0 + v.shape[0]], jnp.concatenate([v, jnp.ones_like(v)], axis=1))
        o = part if o is None else o + part
        k0 += v.shape[0]
    return o[:, :LANE] * (1.0 / o[:, LANE:LANE + 1])


def _mla_heads(q_ref, kts, vs, o_ref):
    for hh in range(MLA_HEADS):
        ks = slice(hh * MLA_DK, (hh + 1) * MLA_DK)
        cs = slice(hh * MLA_V, (hh + 1) * MLA_V)
        o_ref[:, cs] = _softmax_pv(_scores(q_ref[:, ks], kts, ks), vs, cs).astype(BF16)


def _diff_heads(lam_init, lq1_ref, lk1_ref, lq2_ref, lk2_ref, hn_ref, q_ref, kts, vs, o_ref):
    lam = (jnp.exp(jnp.sum(lq1_ref[...] * lk1_ref[...], axis=-1, keepdims=True))
           - jnp.exp(jnp.sum(lq2_ref[...] * lk2_ref[...], axis=-1, keepdims=True)) + lam_init)
    lane = lax.broadcasted_iota(jnp.int32, (1, LANE), 1)
    left = lane < DIFF_HD
    zero = jnp.zeros((), BF16)
    for hh in range(DIFF_HEADS):
        cs = slice(hh * LANE, (hh + 1) * LANE)
        qb = q_ref[:, cs]
        o0 = _softmax_pv(_scores(jnp.where(left, qb, zero), kts, cs), vs, cs)
        o1 = _softmax_pv(_scores(jnp.where(left, zero, qb), kts, cs), vs, cs)
        o = o0 - lam * o1
        o_ref[:, cs] = (_rms(o, hn_ref[...]) * (1.0 - lam_init)).astype(BF16)


def _residual_and_ffn_in(x, mix, mods, g2):
    _, _, gate1, shift2, scale2, _ = mods
    y = x + gate1 * mix
    return y, _modulate(y, g2, shift2, scale2).astype(BF16)


def _attn_kernel(heads, n_x, n_small, *refs):
    x_refs, refs = refs[:n_x], refs[n_x:]
    q_ref, ktp_ref, ktn_ref, ktc_ref, vp_ref, vn_ref, vc_ref = refs[:7]
    smalls, (mod_ref, wo_ref, g2_ref, y_ref, hf_ref, o_ref) = refs[7:7 + n_small], refs[7 + n_small:]
    is_prompt = pl.program_id(0) < NB_P

    @pl.when(is_prompt)
    def _():
        heads(*smalls, q_ref, [ktp_ref], [vp_ref], o_ref)

    @pl.when(jnp.logical_not(is_prompt))
    def _():
        heads(*smalls, q_ref, [ktc_ref, ktn_ref], [vc_ref, vn_ref], o_ref)

    x = x_refs[0][...] if n_x == 1 else _split_rows(*x_refs)
    y, hf = _residual_and_ffn_in(x, _dot(o_ref[...], wo_ref[...]), _mod_parts(mod_ref), g2_ref[...])
    y_ref[...] = y
    hf_ref[...] = hf


def _attention(heads, name, xs, q, kt, v, kt_cache, v_cache, smalls, mods, layer, wo, g2, dk, dv):
    assert SEQ == TM and T_P % DEC_SEQ == 0
    pro = lambda i: jnp.minimum(i, NB_P - 1)
    lat = lambda i: jnp.maximum(i - NB_P, 0) // BPS
    lat_new = lambda i: T_P // DEC_SEQ + lat(i)
    x_specs = [_row_spec(D)] if len(xs) == 1 else _split_specs(D)
    return pl.pallas_call(
        functools.partial(_attn_kernel, heads, len(xs), len(smalls)),
        out_shape=(jax.ShapeDtypeStruct((T, D), F32), jax.ShapeDtypeStruct((T, D), BF16)),
        grid=(NB,),
        in_specs=[*x_specs, _row_spec(dk),
                  pl.BlockSpec((dk, SEQ), lambda i: (0, pro(i))),
                  pl.BlockSpec((dk, DEC_SEQ), lambda i: (0, lat_new(i))),
                  pl.BlockSpec((dk, PAST), lambda i: (0, lat(i))),
                  pl.BlockSpec((SEQ, dv), lambda i: (pro(i), 0)),
                  pl.BlockSpec((DEC_SEQ, dv), lambda i: (lat_new(i), 0)),
                  pl.BlockSpec((PAST, dv), lambda i: (lat(i), 0)),
                  *[_full(s.shape) for s in smalls], _mod_spec(layer), _full((dv, D)), _full((1, D))],
        out_specs=(_row_spec(D), _row_spec(D)),
        scratch_shapes=[pltpu.VMEM((TM, dv), BF16)],
        compiler_params=_params(),
        name=name,
    )(*xs, q, kt, kt, kt_cache, v, v, v_cache, *smalls, mods, wo, g2)


def _zero_row(a, r, cond):
    r0 = r // HALO_F32 * HALO_F32
    tile = a[r0:r0 + HALO_F32]
    row = lax.broadcasted_iota(jnp.int32, (HALO_F32, 1), 0)
    tile = jnp.where(row == r - r0, jnp.where(cond, 0.0, tile), tile)
    return jnp.concatenate([a[:r0], tile, a[r0 + HALO_F32:]], axis=0)


def _conv3(zs_ref, w, off, tm=TM, is_prompt=False):
    z = zs_ref[...]
    n = z.shape[0]
    z_prev = pltpu.roll(z, 1, 0)[off:off + tm]
    z_next = pltpu.roll(z, n - 1, 0)[off:off + tm]
    for start in range(SEQ, tm, SEQ):
        z_prev = _zero_row(z_prev, start, is_prompt)
        z_next = _zero_row(z_next, start - 1, is_prompt)
    return z_prev * w[0:1] + z[off:off + tm] * w[1:2] + z_next * w[2:3]


def _ffn_kernel(n_out, y_ref, hf_ref, prev_ref, next_ref, mod_ref, win_ref, cw_ref, cb_ref, wout_ref,
                *rest):
    out_refs, (lhs_ref, z_ref, act_ref) = rest[:n_out], rest[n_out:]
    tm, hb = FFN_TM, HALO_BF16
    is_prompt = pl.program_id(0) < T_P // tm
    has_prev, has_next = _seq_edges(pl.program_id(0), tm)
    zeros = jnp.zeros((hb, D), BF16)
    lhs_ref[0:hb, :] = jnp.where(has_prev, prev_ref[...], zeros)
    lhs_ref[hb:hb + tm, :] = hf_ref[...]
    lhs_ref[hb + tm:, :] = jnp.where(has_next, next_ref[...], zeros)
    lhs = lhs_ref[...]
    for c in range(FFN_HIDDEN // HC):
        gs = slice(c * HC, (c + 1) * HC)
        us = slice(FFN_HIDDEN + c * HC, FFN_HIDDEN + (c + 1) * HC)
        z_ref[:, gs] = _dot(lhs, win_ref[0, :, gs])
        z_ref[:, us] = _dot(lhs, win_ref[0, :, us])
        g = _conv3(z_ref.at[:, gs], cw_ref[0, :, gs], hb, tm, is_prompt) + cb_ref[0, :, gs]
        u = _conv3(z_ref.at[:, us], cw_ref[0, :, us], hb, tm, is_prompt) + cb_ref[0, :, us]
        act_ref[:, gs] = ((g * _sigmoid(g)) * u).astype(BF16)
    gate2 = _mod_parts(mod_ref)[5]
    out = y_ref[...] + gate2 * _dot(act_ref[...], wout_ref[0])
    if len(out_refs) == 1:
        out_refs[0][...] = out
    else:
        @pl.when(is_prompt)
        def _():
            out_refs[0][...] = out

        @pl.when(jnp.logical_not(is_prompt))
        def _():
            out_refs[1][...] = out


def _ffn(y, hf, mods, layer, win, cw, cb, wout, split_out=False):
    tm, hb = FFN_TM, HALO_BF16
    r = tm // hb
    if split_out:
        out_shape = (jax.ShapeDtypeStruct((T_P, D), F32), jax.ShapeDtypeStruct((T_S, D), F32))
        out_specs = tuple(_split_specs(D, tm))
    else:
        out_shape, out_specs = (jax.ShapeDtypeStruct((T, D), F32),), (_row_spec(D, tm),)
    w_specs = [pl.BlockSpec((1,) + w.shape[1:], lambda i: (layer, 0, 0), pipeline_mode=pl.Buffered(1))
               for w in (win, cw, cb, wout)]
    return pl.pallas_call(
        functools.partial(_ffn_kernel, len(out_shape)),
        out_shape=out_shape,
        grid=(T // tm,),
        in_specs=[_row_spec(D, tm), _row_spec(D, tm),
                  pl.BlockSpec((hb, D), lambda i: (jnp.maximum(i * r - 1, 0), 0)),
                  pl.BlockSpec((hb, D), lambda i: (jnp.minimum((i + 1) * r, T // hb - 1), 0)),
                  _mod_spec(layer, tm), *w_specs],
        out_specs=out_specs,
        scratch_shapes=[pltpu.VMEM((tm + 2 * hb, D), BF16),
                        pltpu.VMEM((tm + 2 * hb, 2 * FFN_HIDDEN), F32),
                        pltpu.VMEM((tm, FFN_HIDDEN), BF16)],
        compiler_params=_params(),
        name="ffn",
    )(y, hf, hf, hf, mods, win, cw, cb, wout)


def _diff_in_kernel(x_ref, mod_ref, g1_ref, w_ref, qn_ref, kn_ref, cos_ref, sin_ref,
                    q_ref, kt_ref, v_ref, kst_ref, vst_ref, kn_ref_scratch):
    shift, scale = _mod_parts(mod_ref)[:2]
    h = _modulate(x_ref[...], g1_ref[...], shift, scale).astype(BF16)
    z = _dot(h, w_ref[...])
    lane = lax.broadcasted_iota(jnp.int32, (1, LANE), 1)
    first = (lane % 64) < 32
    same_half = (lax.broadcasted_iota(jnp.int32, (LANE, LANE), 0) < DIFF_HD) == (
        lax.broadcasted_iota(jnp.int32, (LANE, LANE), 1) < DIFF_HD)
    group_ones = jnp.where(same_half, 1.0, 0.0).astype(BF16)
    cos, sin = cos_ref[...], sin_ref[...]
    v = z[:, 2 * D:]
    v_ref[...] = v.astype(BF16)
    for hh in range(DIFF_HEADS):
        cs = slice(hh * LANE, (hh + 1) * LANE)
        qn = _group_rms(z[:, hh * LANE:(hh + 1) * LANE], qn_ref[...], group_ones)
        q_ref[:, cs] = (_rope128(qn, cos, sin, first) * DIFF_Q_SCALE).astype(BF16)
        kn = _group_rms(z[:, D + hh * LANE:D + (hh + 1) * LANE], kn_ref[...], group_ones)
        kt_ref[hh * LANE:(hh + 1) * LANE, :] = _rope128(kn, cos, sin, first).T.astype(BF16)
        kn_ref_scratch[:, cs] = kn

    @pl.when(pl.program_id(0) < T_P // TMX)
    def _():
        vst_ref[...] = v
        kst_ref[...] = kn_ref_scratch[...]


def _diff_in(x, mods, layer, g1, w, qn, kn, cos_t, sin_t):
    st_spec = _split_specs(D, TMX)[0]
    return pl.pallas_call(
        _diff_in_kernel,
        out_shape=(jax.ShapeDtypeStruct((T, D), BF16), jax.ShapeDtypeStruct((D, T), BF16),
                   jax.ShapeDtypeStruct((T, D), BF16), jax.ShapeDtypeStruct((T_P, D), F32),
                   jax.ShapeDtypeStruct((T_P, D), F32)),
        grid=(T // TMX,),
        in_specs=[_row_spec(D, TMX), _mod_spec(layer, TMX), _full((1, D)), _full((D, 3 * D)),
                  _full((1, LANE)), _full((1, LANE)), _rope_spec(), _rope_spec()],
        out_specs=(_row_spec(D, TMX), pl.BlockSpec((D, TMX), lambda i: (0, i)), _row_spec(D, TMX),
                   st_spec, st_spec),
        scratch_shapes=[pltpu.VMEM((TMX, D), F32)],
        compiler_params=_params(),
        name="diff_in",
    )(x, mods, g1, w, qn, kn, cos_t, sin_t)


def _sconv_kernel(x_ref, prev_ref, next_ref, mod_ref, g1_ref, win_ref, cw_ref, wout_ref, g2_ref,
                  y_ref, hf_ref, p_ref, mixed_ref):
    tm, hb = TMG, HALO_F32
    is_prompt = pl.program_id(0) < T_P // tm
    has_prev, has_next = _seq_edges(pl.program_id(0), tm)
    mods = _mod_parts(mod_ref)
    x = x_ref[...]
    mod_in = lambda a: _modulate(a, g1_ref[...], mods[0], mods[1])
    zeros = jnp.zeros((hb, D), F32)
    lhs = jnp.concatenate([jnp.where(has_prev, mod_in(prev_ref[...]), zeros), mod_in(x),
                           jnp.where(has_next, mod_in(next_ref[...]), zeros)],
                          axis=0).astype(BF16)
    for c in range(D // HC):
        cs = slice(c * HC, (c + 1) * HC)
        gb = _dot(lhs, win_ref[:, c * HC:(c + 1) * HC])
        gc = _dot(lhs, win_ref[:, D + c * HC:D + (c + 1) * HC])
        u = _dot(lhs, win_ref[:, 2 * D + c * HC:2 * D + (c + 1) * HC])
        p_ref[:, cs] = gc * u
        conv = _conv3(p_ref.at[:, cs], cw_ref[:, cs], hb, tm, is_prompt)
        mixed_ref[:, cs] = (gb[hb:hb + tm] * conv).astype(BF16)
    y, hf = _residual_and_ffn_in(x, _dot(mixed_ref[...], wout_ref[...]), mods, g2_ref[...])
    y_ref[...] = y
    hf_ref[...] = hf


def _sconv(x, mods, layer, g1, win, cw, wout, g2):
    tm, hb = TMG, HALO_F32
    r = tm // hb
    once = dict(pipeline_mode=pl.Buffered(1))
    return pl.pallas_call(
        _sconv_kernel,
        out_shape=(jax.ShapeDtypeStruct((T, D), F32), jax.ShapeDtypeStruct((T, D), BF16)),
        grid=(T // tm,),
        in_specs=[_row_spec(D, tm),
                  pl.BlockSpec((hb, D), lambda i: (jnp.maximum(i * r - 1, 0), 0)),
                  pl.BlockSpec((hb, D), lambda i: (jnp.minimum((i + 1) * r, T // hb - 1), 0)),
                  _mod_spec(layer, tm), _full((1, D)),
                  pl.BlockSpec((D, 3 * D), lambda i: (0, 0), **once), _full((3, D)),
                  pl.BlockSpec((D, D), lambda i: (0, 0), **once), _full((1, D))],
        out_specs=(_row_spec(D, tm), _row_spec(D, tm)),
        scratch_shapes=[pltpu.VMEM((tm + 2 * hb, D), F32), pltpu.VMEM((tm, D), BF16)],
        compiler_params=_params(),
        name="sconv",
    )(x, x, x, mods, g1, win, cw, wout, g2)


def _gmlp_kernel(x_ref, mod_ref, g1_ref, win_ref, vn_ref, ws_ref, bs_ref, wout_ref, g2_ref,
                 y_ref, hf_ref, gated_ref):
    mods = _mod_parts(mod_ref)
    x = x_ref[...]
    h = _modulate(x, g1_ref[...], mods[0], mods[1]).astype(BF16)
    z = _dot(h, win_ref[...])
    z = z * (0.5 * (1.0 + jnp.tanh(math.sqrt(2.0 / math.pi) * (z + 0.044715 * (z * z * z)))))
    u = z[:, :D]
    v = _rms(z[:, D:], vn_ref[...]).astype(BF16)
    bs = bs_ref[...]
    for g in range(GMLP_GROUPS):
        cs = slice(g * LANE, (g + 1) * LANE)
        w = ws_ref[g]
        for r in range(TMG // GMLP_CHUNK):
            rs = slice(r * GMLP_CHUNK, (r + 1) * GMLP_CHUNK)
            mixed = _dot(w, v[rs, cs]) + bs[:, g:g + 1]
            gated_ref[rs, cs] = (u[rs, cs] * mixed).astype(BF16)
    y, hf = _residual_and_ffn_in(x, _dot(gated_ref[...], wout_ref[...]), mods, g2_ref[...])
    y_ref[...] = y
    hf_ref[...] = hf


def _gmlp(x, mods, layer, g1, win, vn, ws, bs_t, wout, g2):
    return pl.pallas_call(
        _gmlp_kernel,
        out_shape=(jax.ShapeDtypeStruct((T, D), F32), jax.ShapeDtypeStruct((T, D), BF16)),
        grid=(T // TMG,),
        in_specs=[_row_spec(D, TMG), _mod_spec(layer, TMG), _full((1, D)), _full((D, 2 * D)), _full((1, D)),
                  _full((GMLP_GROUPS, GMLP_CHUNK, GMLP_CHUNK)), _full((GMLP_CHUNK, GMLP_GROUPS)),
                  _full((D, D)), _full((1, D))],
        out_specs=(_row_spec(D, TMG), _row_spec(D, TMG)),
        scratch_shapes=[pltpu.VMEM((TMG, D), BF16)],
        compiler_params=_params(),
        name="gmlp",
    )(x, mods, g1, win, vn, ws, bs_t, wout, g2)


def _rope_tables():
    rows = DEC_SEQ // GRID_W
    row = jnp.repeat(jnp.arange(rows, dtype=F32), GRID_W)
    col = jnp.tile(jnp.arange(GRID_W, dtype=F32), rows)
    n_freq = MLA_ROPE // 4
    inv_freq = ROPE_THETA ** (-jnp.arange(n_freq, dtype=F32) / n_freq)
    ang = jnp.concatenate([row[:, None] * inv_freq, col[:, None] * inv_freq], axis=-1)
    cos, sin = jnp.cos(ang), jnp.sin(ang)
    cos = jnp.concatenate([cos, jnp.ones((TMX, cos.shape[1]), F32)], axis=0)
    sin = jnp.concatenate([sin, jnp.zeros((TMX, sin.shape[1]), F32)], axis=0)
    return jnp.tile(cos, (1, 4)), jnp.tile(jnp.concatenate([-sin, sin], axis=1), (1, 2))


def _pad_lanes(a, width=LANE):
    return jnp.pad(a, [(0, 0)] * (a.ndim - 1) + [(0, width - a.shape[-1])])


def _heads_first(a):
    return jnp.swapaxes(a, -3, -2).reshape(a.shape[:-3] + (D,))


def kernel(x_prompt, x_sample, cache_mla_ckv, cache_mla_krope, cache_diff_k, cache_diff_v, c, c_ctx, ada_w, ada_b, norm1_g, norm2_g, mla_w_down, mla_q_norm, mla_kv_norm, mla_w_uq, mla_w_uk, mla_w_uv, mla_qn_nope, mla_qn_rope, mla_kn_nope, mla_kn_rope, mla_w_o, diff_w_qkv, diff_qn, diff_kn, diff_lq1, diff_lk1, diff_lq2, diff_lk2, diff_head_norm, diff_w_o, sconv_w_in, sconv_w, sconv_w_out, gmlp_w_in, gmlp_v_norm, gmlp_w_s, gmlp_b_s, gmlp_w_out, ffn_w_in, ffn_conv_w, ffn_conv_b, ffn_w_out):
    xp, xs = x_prompt.reshape(T_P, D), x_sample.reshape(T_S, D)
    cond = jnp.concatenate([c_ctx[None], c, jnp.zeros((MOD_ROWS - 1 - DEC_BATCH, D), F32)], axis=0)
    mods = _adaln(cond, ada_w, ada_b).reshape(DEPTH * MOD_ROWS, 1, N_MOD * D)
    cos_t, sin_t = _rope_tables()
    row = lambda a: a.reshape(1, -1)

    ffn_w = (ffn_w_in.astype(BF16), ffn_conv_w, ffn_conv_b[:, None, :], ffn_w_out.astype(BF16))

    def ffn(y, hf, i):
        return _ffn(y, hf, mods, i, *ffn_w, split_out=(i == DEPTH - 1))

    wd = _pad_lanes(mla_w_down[0], MLA_DOWN_W).astype(BF16)
    wuq = _pad_lanes(mla_w_uq[0].reshape(MLA_Q_LORA, MLA_HEADS, MLA_NOPE + MLA_ROPE), MLA_DK)
    wuq = wuq.reshape(MLA_Q_LORA, MLA_HEADS * MLA_DK).astype(BF16)
    q, ckv, kr = _mla_in(xp, xs, mods, 0, row(norm1_g[0]), wd, row(mla_q_norm[0]), row(mla_kv_norm[0]),
                         _pad_lanes(row(mla_kn_rope[0])), wuq, row(mla_qn_nope[0]),
                         _pad_lanes(row(mla_qn_rope[0])), cos_t, sin_t)
    state_ckv = ckv[:T_P].reshape(BATCH, 1, SEQ, MLA_KV_LORA)
    state_kr = kr[:T_P, :MLA_ROPE].reshape(BATCH, 1, SEQ, MLA_ROPE)
    kv_w = (mla_w_uk[0].astype(BF16), mla_w_uv[0].astype(BF16), row(mla_kn_nope[0]))
    kt, v = _mla_kv(ckv, kr, *kv_w)
    kt_cache, v_cache = _mla_kv(cache_mla_ckv[:, 0].reshape(DEC_BATCH * PAST, MLA_KV_LORA),
                                _pad_lanes(cache_mla_krope[:, 0]).reshape(DEC_BATCH * PAST, LANE), *kv_w)
    y, hf = _attention(_mla_heads, "mla_attn", (xp, xs), q, kt, v, kt_cache, v_cache, [], mods, 0,
                       mla_w_o[0].astype(BF16), row(norm2_g[0]), MLA_HEADS * MLA_DK, MLA_HEADS * MLA_V)
    x, = ffn(y, hf, 0)

    lam_init = 0.8 - 0.6 * math.exp(-0.3 * 1)
    wq, wk, wv = jnp.split(diff_w_qkv[0], 3, axis=-1)
    perm = lambda w: _heads_first(w.reshape(D, 2, DIFF_HEADS, DIFF_HD))
    w_qkv = jnp.concatenate([perm(wq), perm(wk), wv], axis=-1).astype(BF16)
    pair = lambda g: row(jnp.tile(g, 2))
    q, kt, v, kst, vst = _diff_in(x, mods, 1, row(norm1_g[1]), w_qkv, pair(diff_qn[0]), pair(diff_kn[0]),
                                 cos_t, sin_t)
    state_dk = jnp.swapaxes(kst.reshape(BATCH, SEQ, DIFF_HEADS, 2, DIFF_HD), 2, 3)[:, None]
    state_dv = vst.reshape(BATCH, 1, SEQ, DIFF_HEADS, 2 * DIFF_HD)
    kt_cache = _heads_first(cache_diff_k[:, 0]).astype(BF16).reshape(DEC_BATCH * PAST, D).T
    v_cache = cache_diff_v[:, 0].reshape(DEC_BATCH * PAST, D).astype(BF16)
    smalls = [row(diff_lq1[0]), row(diff_lk1[0]), row(diff_lq2[0]), row(diff_lk2[0]),
              row(diff_head_norm[0])]
    y, hf = _attention(functools.partial(_diff_heads, lam_init), "diff_attn", (x,), q, kt, v, kt_cache, v_cache,
                       smalls, mods, 1, diff_w_o[0].astype(BF16), row(norm2_g[1]), D, D)
    x, = ffn(y, hf, 1)

    y, hf = _sconv(x, mods, 2, row(norm1_g[2]), sconv_w_in[0].astype(BF16), sconv_w[0],
                   sconv_w_out[0].astype(BF16), row(norm2_g[2]))
    x, = ffn(y, hf, 2)

    y, hf = _gmlp(x, mods, 3, row(norm1_g[3]), gmlp_w_in[0].astype(BF16), row(gmlp_v_norm[0]),
                  gmlp_w_s[0].astype(BF16), gmlp_b_s[0].T, gmlp_w_out[0].astype(BF16), row(norm2_g[3]))
    yp, ys = ffn(y, hf, 3)

    return (yp.reshape(BATCH, SEQ, D), ys.reshape(DEC_BATCH, DEC_SEQ, D),
            state_ckv, state_kr, state_dk, state_dv)
```
